```python
import math
import jax, jax.numpy as jnp
from jax import lax
import numpy as np

D_MODEL = 2048
BATCH = 4
SEQ = 2048
DEPTH = 4
DEC_BATCH = 8
DEC_SEQ = 4
PAST_LEN = 16384
PAGE_SIZE = 128

D_MIX = D_MODEL
N_MIXERS = 4
W_GRP = D_MIX // N_MIXERS
HEAD_DIM = 64
SB_HEADS = W_GRP // HEAD_DIM
NSA_HEADS = W_GRP // HEAD_DIM
NSA_KV_HEADS = 2
NSA_KV = NSA_KV_HEADS * HEAD_DIM
L_CMP = 32
L_SEL = 64
N_SEL = 16
WINDOW = 512
FORCE_BONUS = 1000.0
POOL_WINDOWS = (2, 4, 8, 16)
POOL_GROUPS = 4
POOL_CH = W_GRP // POOL_GROUPS
POOL_BUF = 15
CONV_W = 31
CONV_BUF = CONV_W - 1
Q_BLOCK = 128
RMS_EPS = 1e-6
LN_EPS = 1e-5
NEG_BIG = -1e30
SPLIT_SIZES = (W_GRP, W_GRP, W_GRP, W_GRP,
               W_GRP, NSA_KV, NSA_KV, NSA_KV, NSA_KV, NSA_KV, NSA_KV, W_GRP, NSA_HEADS * 3,
               W_GRP, W_GRP,
               2 * W_GRP, W_GRP)
N_IN = sum(SPLIT_SIZES)

kernel_name = 'hybrid_sb_nsa_pool_conv_decoder_step'


def rms_norm(x, g):
    x32 = x.astype(jnp.float32)
    y = x32 * lax.rsqrt(jnp.mean(x32 * x32, axis=-1, keepdims=True) + RMS_EPS)
    return (y * g.astype(jnp.float32)).astype(x.dtype)


def layer_norm(x, g, b):
    x32 = x.astype(jnp.float32)
    xc = x32 - jnp.mean(x32, axis=-1, keepdims=True)
    y = xc * lax.rsqrt(jnp.mean(xc * xc, axis=-1, keepdims=True) + LN_EPS)
    return (y * g.astype(jnp.float32) + b.astype(jnp.float32)).astype(x.dtype)


def masked_softmax(s, mask):
    p = jax.nn.softmax(jnp.where(mask, s, NEG_BIG), axis=-1)
    return jnp.where(mask, p, 0.0)


def alibi_slopes(n_heads):
    return jnp.exp2(-8.0 * jnp.arange(1, n_heads + 1, dtype=jnp.float32) / n_heads)


def gather_pages(pool, page_table):
    rows = pool[page_table]
    db, n_pages, page, h, d = rows.shape
    return rows.reshape(db, n_pages * page, h, d)


def stick_breaking(q, k, v):
    b, tq, h, d = q.shape
    t_all = k.shape[1]
    qb = math.gcd(Q_BLOCK, tq)
    nb = tq // qb
    q_blocks = q.reshape(b, nb, qb, h, d).transpose(1, 0, 2, 3, 4)
    kpos = jnp.arange(t_all)
    scale = d ** -0.5

    def one_block(args):
        i, qblk = args
        qpos = t_all - tq + i * qb + jnp.arange(qb)
        z = jnp.einsum('bqhd,bkhd->bhqk', qblk, k, preferred_element_type=jnp.float32) * scale
        before = kpos[None, :] < qpos[:, None]
        log_keep = jnp.where(before, jax.nn.log_sigmoid(-z), 0.0)
        between = lax.cumsum(log_keep, axis=3, reverse=True) - log_keep
        a = jnp.where(before, jnp.exp(jax.nn.log_sigmoid(z) + between), 0.0)
        return jnp.einsum('bhqk,bkhd->bqhd', a.astype(v.dtype), v)

    out = lax.map(one_block, (jnp.arange(nb), q_blocks))
    return out.transpose(1, 0, 2, 3, 4).reshape(b, tq, h * d)


def nsa_attention(q, kc, vc, ks, vs, kw_ext, vw_ext, gates, w_ck, w_cv):
    b, tq, h, d = q.shape
    t_all, kvh = kc.shape[1], kc.shape[2]
    grp = h // kvh
    q_start = t_all - tq
    n_before = kw_ext.shape[1] - tq
    scale = d ** -0.5
    slopes = alibi_slopes(h).reshape(kvh, grp)[None, :, :, None, None]
    n_cmp = t_all // L_CMP
    k_cmp = jnp.einsum('bnlhd,lde->bnhe', kc[:, :n_cmp * L_CMP].reshape(b, n_cmp, L_CMP, kvh, d), w_ck)
    v_cmp = jnp.einsum('bnlhd,lde->bnhe', vc[:, :n_cmp * L_CMP].reshape(b, n_cmp, L_CMP, kvh, d), w_cv)
    cmp_end = (jnp.arange(n_cmp) + 1) * L_CMP - 1
    n_sel = -(-t_all // L_SEL)
    n_top = min(N_SEL, n_sel)
    ratio = L_SEL // L_CMP
    pad = n_sel * L_SEL - t_all

    def to_blocks(t):
        t = jnp.pad(t, ((0, 0), (0, pad), (0, 0), (0, 0)))
        return t.reshape(b, n_sel, L_SEL, kvh, d).transpose(0, 3, 1, 2, 4)

    ks_b, vs_b = to_blocks(ks), to_blocks(vs)
    bi = jnp.arange(b)[:, None, None, None]
    hi = jnp.arange(kvh)[None, :, None, None]
    blk_idx = jnp.arange(n_sel)
    qb = math.gcd(Q_BLOCK, tq)
    nb = tq // qb
    q_blocks = q.reshape(b, nb, qb, kvh, grp, d).transpose(1, 0, 2, 3, 4, 5)
    g_blocks = gates.reshape(b, nb, qb, kvh, grp, 3).transpose(1, 0, 2, 3, 4, 5)
    n_kw = n_before + qb

    def one_block(args):
        i, qblk, gblk = args
        qpos = q_start + i * qb + jnp.arange(qb)
        dist_c = (qpos[:, None] - cmp_end[None, :]).astype(jnp.float32)
        s_c = jnp.einsum('bqkgd,bnkd->bkgqn', qblk, k_cmp, preferred_element_type=jnp.float32) * scale - slopes * dist_c
        p_c = masked_softmax(s_c, dist_c >= 0)
        o_c = jnp.einsum('bkgqn,bnkd->bqkgd', p_c.astype(v_cmp.dtype), v_cmp)
        imp = jnp.pad(p_c.sum(axis=2), ((0, 0), (0, 0), (0, 0), (0, n_sel * ratio - n_cmp)))
        imp = imp.reshape(b, kvh, qb, n_sel, ratio).sum(-1)
        cur = qpos[:, None] // L_SEL
        forced = (blk_idx[None, :] == 0) | (blk_idx[None, :] == cur) | (blk_idx[None, :] == cur - 1)
        valid = blk_idx[None, :] * L_SEL <= qpos[:, None]
        score = jnp.where(valid, imp + jnp.where(forced, FORCE_BONUS, 0.0), -jnp.inf)
        top_s, top_i = lax.top_k(score, n_top)
        kb = ks_b[bi, hi, top_i].reshape(b, kvh, qb, n_top * L_SEL, d)
        vb = vs_b[bi, hi, top_i].reshape(b, kvh, qb, n_top * L_SEL, d)
        kpos_s = (top_i[..., None] * L_SEL + jnp.arange(L_SEL)).reshape(b, kvh, 1, qb, n_top * L_SEL)
        ok_s = jnp.broadcast_to(jnp.isfinite(top_s)[..., None], (b, kvh, qb, n_top, L_SEL)).reshape(b, kvh, 1, qb, n_top * L_SEL)
        dist_s = (qpos[None, None, None, :, None] - kpos_s).astype(jnp.float32)
        s_s = jnp.einsum('bqkgd,bkqmd->bkgqm', qblk, kb, preferred_element_type=jnp.float32) * scale - slopes * dist_s
        p_s = masked_softmax(s_s, ok_s & (dist_s >= 0))
        o_s = jnp.einsum('bkgqm,bkqmd->bqkgd', p_s.astype(vb.dtype), vb)
        kw = lax.dynamic_slice_in_dim(kw_ext, i * qb, n_kw, axis=1)
        vw = lax.dynamic_slice_in_dim(vw_ext, i * qb, n_kw, axis=1)
        kpos_w = q_start - n_before + i * qb + jnp.arange(n_kw)
        dist_w = qpos[:, None] - kpos_w[None, :]
        ok_w = (kpos_w[None, :] >= 0) & (dist_w >= 0) & (dist_w < WINDOW)
        s_w = jnp.einsum('bqkgd,bskd->bkgqs', qblk, kw, preferred_element_type=jnp.float32) * scale - slopes * dist_w.astype(jnp.float32)
        p_w = masked_softmax(s_w, ok_w)
        o_w = jnp.einsum('bkgqs,bskd->bqkgd', p_w.astype(vw.dtype), vw)
        return gblk[..., 0:1] * o_c + gblk[..., 1:2] * o_s + gblk[..., 2:3] * o_w

    out = lax.map(one_block, (jnp.arange(nb), q_blocks, g_blocks))
    return out.transpose(1, 0, 2, 3, 4, 5).reshape(b, tq, h * d)


def pool_mixer(u, prefix, start, w_pool_l, pool_scale_l):
    b, tq, c = u.shape
    p = prefix.shape[1]
    ext = jnp.concatenate([prefix, u], axis=1)
    csum = jnp.concatenate([jnp.zeros((b, 1, c), jnp.float32), jnp.cumsum(ext.astype(jnp.float32), axis=1)], axis=1)
    win = jnp.repeat(jnp.array(POOL_WINDOWS, jnp.int32), POOL_CH)
    end = p + 1 + jnp.arange(tq)
    chan = jnp.arange(c)[None, :]
    total = csum[:, end[:, None], chan] - csum[:, end[:, None] - win[None, :], chan]
    pos = start + jnp.arange(tq)
    cnt = jnp.minimum(win[None, :], pos[:, None] + 1).astype(jnp.float32)
    diff = (total / cnt - u.astype(jnp.float32)).astype(u.dtype)
    y = jnp.einsum('btgc,gce->btge', diff.reshape(b, tq, POOL_GROUPS, POOL_CH), w_pool_l).reshape(b, tq, c)
    return y * pool_scale_l, ext[:, -p:]


def conformer_conv(glu_in, prefix, conv_w_l, conv_b_l, ln_g_l, ln_b_l, w_pw_l):
    a, g = jnp.split(glu_in, 2, axis=-1)
    u = a * jax.nn.sigmoid(g)
    c = u.shape[-1]
    ext = jnp.concatenate([prefix, u], axis=1)
    y = lax.conv_general_dilated(ext, conv_w_l[:, None, :], window_strides=(1,), padding='VALID',
                                 dimension_numbers=('NWC', 'WIO', 'NWC'), feature_group_count=c) + conv_b_l
    y = jax.nn.silu(layer_norm(y, ln_g_l, ln_b_l))
    return y @ w_pw_l, ext[:, -CONV_BUF:]


def mixer_layer(h, past, params, win_keep):
    (sb_k0, sb_v0, cmp_k0, cmp_v0, sel_k0, sel_v0, win_k0, win_v0, pool0, conv0) = past
    (g_norm, w_in_l, w_ck, w_cv, w_pool_l, pool_scale_l, conv_w_l, conv_b_l, ln_g_l, ln_b_l, w_pw_l, w_out_l) = params
    b, tq, _ = h.shape
    start = sb_k0.shape[1]
    hn = rms_norm(h, g_norm)
    points = np.cumsum(SPLIT_SIZES)[:-1].tolist()
    (a_q, a_k, a_v, a_g, b_q, b_kc, b_vc, b_ks, b_vs, b_kw, b_vw, b_g, b_br,
     c_u, c_g, d_in, d_g) = jnp.split(hn @ w_in_l, points, axis=-1)
    heads = lambda t, n: t.reshape(b, tq, n, HEAD_DIM)
    sb_k, sb_v = heads(a_k, SB_HEADS), heads(a_v, SB_HEADS)
    o_a = stick_breaking(heads(a_q, SB_HEADS), jnp.concatenate([sb_k0, sb_k], axis=1), jnp.concatenate([sb_v0, sb_v], axis=1))
    kc, vc, ks, vs, kw, vw = [heads(t, NSA_KV_HEADS) for t in (b_kc, b_vc, b_ks, b_vs, b_kw, b_vw)]
    kw_ext = jnp.concatenate([win_k0, kw], axis=1)
    vw_ext = jnp.concatenate([win_v0, vw], axis=1)
    gates = jax.nn.sigmoid(b_br).reshape(b, tq, NSA_KV_HEADS, NSA_HEADS // NSA_KV_HEADS, 3)
    o_b = nsa_attention(heads(b_q, NSA_HEADS), jnp.concatenate([cmp_k0, kc], axis=1), jnp.concatenate([cmp_v0, vc], axis=1),
                        jnp.concatenate([sel_k0, ks], axis=1), jnp.concatenate([sel_v0, vs], axis=1),
                        kw_ext, vw_ext, gates, w_ck, w_cv)
    o_c, pool_new = pool_mixer(c_u, pool0, start, w_pool_l, pool_scale_l)
    o_d, conv_new = conformer_conv(d_in, conv0, conv_w_l, conv_b_l, ln_g_l, ln_b_l, w_pw_l)
    mix = jnp.concatenate([o_a * jax.nn.silu(a_g), o_b * jax.nn.silu(b_g),
                           o_c * jax.nn.silu(c_g), o_d * jax.nn.silu(d_g)], axis=-1)
    h = h + mix @ w_out_l
    new_state = (sb_k, sb_v, kc, vc, ks, vs, kw_ext[:, -win_keep:], vw_ext[:, -win_keep:], pool_new, conv_new)
    return h, new_state


def setup_inputs(seed: int = 0) -> dict:
    key = jax.random.key(seed)
    ks = jax.random.split(key, 26)
    n_pages = PAST_LEN // PAGE_SIZE
    n_used = DEC_BATCH * n_pages
    n_pool = n_used + max(1, n_used // 4)
    w_buf = min(WINDOW, PAST_LEN)

    def nrm(k, shape, s=1.0):
        return jax.random.normal(k, shape, jnp.float32) * s

    page_table = jax.random.permutation(ks[12], n_pool)[:n_used].reshape(DEC_BATCH, n_pages).astype(jnp.int32)
    return {
        'x_prompt': nrm(ks[0], (BATCH, SEQ, D_MODEL)),
        'x_sample': nrm(ks[1], (DEC_BATCH, DEC_SEQ, D_MODEL)),
        'cache_sb_k': nrm(ks[2], (DEPTH, n_pool, PAGE_SIZE, SB_HEADS, HEAD_DIM)),
        'cache_sb_v': nrm(ks[3], (DEPTH, n_pool, PAGE_SIZE, SB_HEADS, HEAD_DIM)),
        'cache_cmp_k': nrm(ks[4], (DEPTH, n_pool, PAGE_SIZE, NSA_KV_HEADS, HEAD_DIM)),
        'cache_cmp_v': nrm(ks[5], (DEPTH, n_pool, PAGE_SIZE, NSA_KV_HEADS, HEAD_DIM)),
        'cache_sel_k': nrm(ks[6], (DEPTH, n_pool, PAGE_SIZE, NSA_KV_HEADS, HEAD_DIM)),
        'cache_sel_v': nrm(ks[7], (DEPTH, n_pool, PAGE_SIZE, NSA_KV_HEADS, HEAD_DIM)),
        'cache_win_k': nrm(ks[8], (DEPTH, DEC_BATCH, w_buf, NSA_KV_HEADS, HEAD_DIM)),
        'cache_win_v': nrm(ks[9], (DEPTH, DEC_BATCH, w_buf, NSA_KV_HEADS, HEAD_DIM)),
        'state_pool': nrm(ks[10], (DEPTH, DEC_BATCH, POOL_BUF, W_GRP)),
        'state_conv': nrm(ks[11], (DEPTH, DEC_BATCH, CONV_BUF, W_GRP), 0.5),
        'page_table': page_table,
        'norm_g': 1.0 + nrm(ks[13], (DEPTH, D_MODEL), 0.05),
        'w_in': nrm(ks[14], (DEPTH, D_MODEL, N_IN), D_MODEL ** -0.5),
        'w_cmp_k': nrm(ks[15], (DEPTH, L_CMP, HEAD_DIM, HEAD_DIM), (L_CMP * HEAD_DIM) ** -0.5),
        'w_cmp_v': nrm(ks[16], (DEPTH, L_CMP, HEAD_DIM, HEAD_DIM), (L_CMP * HEAD_DIM) ** -0.5),
        'w_pool': nrm(ks[17], (DEPTH, POOL_GROUPS, POOL_CH, POOL_CH), POOL_CH ** -0.5),
        'pool_scale': 1.0 + nrm(ks[18], (DEPTH, W_GRP), 0.1),
        'conv_w': nrm(ks[19], (DEPTH, CONV_W, W_GRP), CONV_W ** -0.5),
        'conv_b': nrm(ks[20], (DEPTH, W_GRP), 0.02),
        'ln_g': 1.0 + nrm(ks[21], (DEPTH, W_GRP), 0.05),
        'ln_b': nrm(ks[22], (DEPTH, W_GRP), 0.02),
        'w_pw': nrm(ks[23], (DEPTH, W_GRP, W_GRP), W_GRP ** -0.5),
        'w_out': nrm(ks[24], (DEPTH, D_MIX, D_MODEL), D_MIX ** -0.5),
        'final_g': 1.0 + nrm(ks[25], (D_MODEL,), 0.05),
    }


def reference(x_prompt, x_sample, cache_sb_k, cache_sb_v, cache_cmp_k, cache_cmp_v, cache_sel_k, cache_sel_v,
              cache_win_k, cache_win_v, state_pool, state_conv, page_table, norm_g, w_in, w_cmp_k, w_cmp_v,
              w_pool, pool_scale, conv_w, conv_b, ln_g, ln_b, w_pw, w_out, final_g):
    bp, seq, _ = x_prompt.shape
    dt = x_prompt.dtype

    def zeros(shape):
        return jnp.zeros(shape, dt)

    prompt_past = (zeros((bp, 0, SB_HEADS, HEAD_DIM)), zeros((bp, 0, SB_HEADS, HEAD_DIM)),
                   zeros((bp, 0, NSA_KV_HEADS, HEAD_DIM)), zeros((bp, 0, NSA_KV_HEADS, HEAD_DIM)),
                   zeros((bp, 0, NSA_KV_HEADS, HEAD_DIM)), zeros((bp, 0, NSA_KV_HEADS, HEAD_DIM)),
                   zeros((bp, WINDOW, NSA_KV_HEADS, HEAD_DIM)), zeros((bp, WINDOW, NSA_KV_HEADS, HEAD_DIM)),
                   zeros((bp, POOL_BUF, W_GRP)), zeros((bp, CONV_BUF, W_GRP)))
    prompt_keep = min(WINDOW, seq)
    sample_keep = cache_win_k.shape[2]
    hp, hs = x_prompt, x_sample
    p_states, s_states = [], []
    for l in range(DEPTH):
        params = (norm_g[l], w_in[l], w_cmp_k[l], w_cmp_v[l], w_pool[l], pool_scale[l],
                  conv_w[l], conv_b[l], ln_g[l], ln_b[l], w_pw[l], w_out[l])
        hp, st_p = mixer_layer(hp, prompt_past, params, prompt_keep)
        p_states.append(st_p)
        sample_past = (gather_pages(cache_sb_k[l], page_table), gather_pages(cache_sb_v[l], page_table),
                       gather_pages(cache_cmp_k[l], page_table), gather_pages(cache_cmp_v[l], page_table),
                       gather_pages(cache_sel_k[l], page_table), gather_pages(cache_sel_v[l], page_table),
                       cache_win_k[l], cache_win_v[l], state_pool[l], state_conv[l])
        hs, st_s = mixer_layer(hs, sample_past, params, sample_keep)
        s_states.append(st_s)
    y_prompt = rms_norm(hp, final_g)
    y_sample = rms_norm(hs, final_g)

    def stacked(states, j):
        return jnp.stack([st[j] for st in states], axis=0)

    p_sb_k, p_sb_v = stacked(p_states, 0), stacked(p_states, 1)
    p_cmp_k, p_cmp_v = stacked(p_states, 2), stacked(p_states, 3)
    p_sel_k, p_sel_v = stacked(p_states, 4), stacked(p_states, 5)
    p_win_k, p_win_v = stacked(p_states, 6), stacked(p_states, 7)
    p_pool, p_conv = stacked(p_states, 8), stacked(p_states, 9)
    s_sb_k, s_sb_v = stacked(s_states, 0), stacked(s_states, 1)
    s_cmp_k, s_cmp_v = stacked(s_states, 2), stacked(s_states, 3)
    s_sel_k, s_sel_v = stacked(s_states, 4), stacked(s_states, 5)
    s_win_k, s_win_v = stacked(s_states, 6), stacked(s_states, 7)
    s_pool, s_conv = stacked(s_states, 8), stacked(s_states, 9)
    return (y_prompt, y_sample,
            p_sb_k, p_sb_v, p_cmp_k, p_cmp_v, p_sel_k, p_sel_v, p_win_k, p_win_v, p_pool, p_conv,
            s_sb_k, s_sb_v, s_cmp_k, s_cmp_v, s_sel_k, s_sel_v, s_win_k, s_win_v, s_pool, s_conv)
```

```python
import functools

import jax
import jax.numpy as jnp
from jax import lax
from jax.experimental import pallas as pl
from jax.experimental.pallas import tpu as pltpu

F32 = jnp.float32
BF16 = jnp.bfloat16

HEAD_DIM = 64
LANES = 128
W_GRP = 512
L_CMP = 32
L_SEL = 64
N_SEL = 16
WINDOW = 512
FORCE_BONUS = 1000.0
POOL_WINDOWS = (2, 4, 8, 16)
POOL_HALO = 16
CONV_W = 31
CONV_HALO = 32
RMS_EPS = 1e-6
LN_EPS = 1e-5
NEG_BIG = -1e30
SCALE = HEAD_DIM ** -0.5

COL = dict(a_q=0, a_k=512, a_v=1024, a_g=1536, b_q=2048, b_g=2560, c_u=3072, c_g=3584,
           d_a=4096, d_gl=4608, d_g=5120, b_kc=5632, b_vc=5760, b_ks=5888, b_vs=6016,
           b_kw=6144, b_vw=6272, b_br=6400)
N_PACK = 6656
_SRC = (("a_q", 512), ("a_k", 512), ("a_v", 512), ("a_g", 512), ("b_q", 512), ("b_kc", 128),
        ("b_vc", 128), ("b_ks", 128), ("b_vs", 128), ("b_kw", 128), ("b_vw", 128), ("b_g", 512),
        ("b_br", 24), ("c_u", 512), ("c_g", 512), ("d_a", 512), ("d_gl", 512), ("d_g", 512))


def _params(sem, vmem_mb=None):
    kw = dict(dimension_semantics=sem)
    if vmem_mb is not None:
        kw["vmem_limit_bytes"] = vmem_mb * 2 ** 20
    return pltpu.CompilerParams(**kw)


def _sigmoid(x):
    return 1.0 / (1.0 + jnp.exp(-x))


def _iota(shape, axis):
    return lax.broadcasted_iota(jnp.int32, shape, axis)


def _dot_nt(a, b):
    return lax.dot_general(a, b, (((1,), (1,)), ((), ())), preferred_element_type=F32)


def _dot(a, b):
    return jnp.dot(a, b, preferred_element_type=F32)


def _split_dot(x, m):
    hi = x.astype(BF16)
    lo = (x - hi.astype(F32)).astype(BF16)
    return _dot(hi, m) + _dot(lo, m)


def _log_keep(z):
    return jnp.minimum(-z, 0.0) - jnp.log1p(jnp.exp(-jnp.abs(z)))


def _upper_ones(n):
    return jnp.where(_iota((n, n), 0) > _iota((n, n), 1), 1.0, 0.0).astype(BF16)


def _inproj_kernel(x_ref, g_ref, w_ref, o_ref, hn_ref):
    @pl.when(pl.program_id(1) == 0)
    def _():
        x = x_ref[...]
        ms = jnp.mean(x * x, axis=-1, keepdims=True)
        hn_ref[...] = ((x * lax.rsqrt(ms + RMS_EPS)) * g_ref[...]).astype(BF16)

    o_ref[...] = _dot(hn_ref[...], w_ref[...])


def _inproj(x, norm_g, w_pack, layer):
    rows, d = x.shape
    n = w_pack.shape[2]
    tm = min(rows, 1024)
    tn = 512
    return pl.pallas_call(
        _inproj_kernel,
        grid=(rows // tm, n // tn),
        in_specs=[pl.BlockSpec((tm, d), lambda i, j: (i, 0)),
                  pl.BlockSpec((None, 1, d), lambda i, j: (layer, 0, 0)),
                  pl.BlockSpec((None, d, tn), lambda i, j: (layer, 0, j))],
        out_specs=pl.BlockSpec((tm, tn), lambda i, j: (i, j)),
        out_shape=jax.ShapeDtypeStruct((rows, n), F32),
        scratch_shapes=[pltpu.VMEM((tm, d), BF16)],
        compiler_params=_params(("parallel", "arbitrary"), 48),
        name="inproj",
    )(x, norm_g, w_pack)


def _outproj_kernel(oa, ob, oc, od, ga, gb, gc, gd, h_ref, w_ref, *rest, final):
    if final:
        fg_ref, o_ref = rest
    else:
        (o_ref,) = rest
    acc = h_ref[...]
    for i, (o, g) in enumerate(((oa, ga), (ob, gb), (oc, gc), (od, gd))):
        gv = g[...]
        mix = (o[...] * (gv * _sigmoid(gv))).astype(BF16)
        acc = acc + _dot(mix, w_ref[i * W_GRP:(i + 1) * W_GRP, :])
    if final:
        ms = jnp.mean(acc * acc, axis=-1, keepdims=True)
        acc = (acc * lax.rsqrt(ms + RMS_EPS)) * fg_ref[...]
    o_ref[...] = acc


def _outproj(outs, proj, h, w_out, layer, final_g=None):
    rows, d = h.shape
    tm = min(rows, 256)
    final = final_g is not None
    gate_blocks = [COL[k] // W_GRP for k in ("a_g", "b_g", "c_g", "d_g")]
    in_specs = [pl.BlockSpec((tm, W_GRP), lambda i: (i, 0)) for _ in range(4)]
    in_specs += [pl.BlockSpec((tm, W_GRP), functools.partial(lambda i, cb: (i, cb), cb=cb))
                 for cb in gate_blocks]
    in_specs += [pl.BlockSpec((tm, d), lambda i: (i, 0)),
                 pl.BlockSpec((None, 4 * W_GRP, d), lambda i: (layer, 0, 0))]
    args = list(outs) + [proj] * 4 + [h, w_out]
    if final:
        in_specs.append(pl.BlockSpec((1, d), lambda i: (0, 0)))
        args.append(final_g)
    return pl.pallas_call(
        functools.partial(_outproj_kernel, final=final),
        grid=(rows // tm,),
        in_specs=in_specs,
        out_specs=pl.BlockSpec((tm, d), lambda i: (i, 0)),
        out_shape=jax.ShapeDtypeStruct((rows, d), F32),
        compiler_params=_params(("parallel",), 48),
        name="outproj",
    )(*args)


def _sb_prompt_kernel(q_ref, k_ref, v_ref, o_ref, *, tq):
    qt = pl.program_id(2)
    lane = _iota((tq, LANES), 1)
    row = _iota((tq, tq), 0)
    col = _iota((tq, tq), 1)
    before = col < row
    upper = _upper_ones(tq)
    q = q_ref[0] * SCALE
    outs = []
    for hh in range(2):
        in_head = (lane >= hh * HEAD_DIM) & (lane < (hh + 1) * HEAD_DIM)
        qm = jnp.where(in_head, q, 0.0).astype(BF16)

        def tile(kt, carry, acc, diag, qm=qm):
            start = pl.multiple_of(kt * tq, tq)
            k = k_ref[0, pl.ds(start, tq), :].astype(BF16)
            v = v_ref[0, pl.ds(start, tq), :].astype(BF16)
            z = _dot_nt(qm, k)
            lk = _log_keep(z)
            if diag:
                lk = jnp.where(before, lk, 0.0)
            between = _split_dot(lk, upper) + carry
            a = jnp.exp(z + lk + between)
            if diag:
                a = jnp.where(before, a, 0.0)
            acc = acc + _dot(a.astype(BF16), v)
            carry = carry + jnp.sum(lk, axis=1, keepdims=True)
            return carry, acc

        carry, acc = tile(qt, jnp.zeros((tq, 1), F32), jnp.zeros((tq, LANES), F32), True)
        carry, acc = lax.fori_loop(
            0, qt, lambda i, c: tile(qt - 1 - i, c[0], c[1], False), (carry, acc))
        outs.append(acc)
    o_ref[0] = jnp.where(lane < HEAD_DIM, outs[0], outs[1])


def _sb_prompt(proj3):
    b, t, _ = proj3.shape
    tq = min(t, 256)
    qb, kb, vb = (COL[k] // LANES for k in ("a_q", "a_k", "a_v"))
    return pl.pallas_call(
        functools.partial(_sb_prompt_kernel, tq=tq),
        grid=(b, W_GRP // LANES, t // tq),
        in_specs=[pl.BlockSpec((1, tq, LANES), lambda bi, p, i: (bi, i, qb + p)),
                  pl.BlockSpec((1, t, LANES), lambda bi, p, i: (bi, 0, kb + p)),
                  pl.BlockSpec((1, t, LANES), lambda bi, p, i: (bi, 0, vb + p))],
        out_specs=pl.BlockSpec((1, tq, LANES), lambda bi, p, i: (bi, i, p)),
        out_shape=jax.ShapeDtypeStruct((b, t, W_GRP), F32),
        compiler_params=_params(("parallel", "parallel", "arbitrary")),
        name="sb_prompt",
    )(proj3, proj3, proj3)


def _cmp_kernel(xk_ref, xv_ref, wk_ref, wv_ref, ok_ref, ov_ref):
    ok_ref[...] = _dot(xk_ref[...].astype(BF16), wk_ref[...])
    ov_ref[...] = _dot(xv_ref[...].astype(BF16), wv_ref[...])


def _compress(xk, xv, wk_exp, wv_exp, layer, row0, rows):
    kdim = xk.shape[1]
    tr = min(rows, 256)
    off = row0 // tr
    return pl.pallas_call(
        _cmp_kernel,
        grid=(rows // tr,),
        in_specs=[pl.BlockSpec((tr, kdim), lambda i: (off + i, 0)),
                  pl.BlockSpec((tr, kdim), lambda i: (off + i, 0)),
                  pl.BlockSpec((None, kdim, LANES), lambda i: (layer, 0, 0)),
                  pl.BlockSpec((None, kdim, LANES), lambda i: (layer, 0, 0))],
        out_specs=[pl.BlockSpec((tr, LANES), lambda i: (i, 0))] * 2,
        out_shape=[jax.ShapeDtypeStruct((rows, LANES), F32)] * 2,
        compiler_params=_params(("parallel",), 48),
        name="compress",
    )(xk, xv, wk_exp, wv_exp)


def _head_slope(h):
    return 2.0 ** -(h + 1)


def _half_variant(x, x_swapped, kvh, par, low):
    src = x if kvh == par else x_swapped
    keep = low if par == 0 else jnp.logical_not(low)
    return jnp.where(keep, src, 0.0).astype(BF16)


def _nsa_cmp_kernel(q_ref, kc_ref, vc_ref, oc_ref, sel_ref, *, tq, ncmp):
    qt = pl.program_id(1)
    nsel = ncmp // 2
    ntop = min(N_SEL, nsel)
    half = LANES // 2

    def permuted(ref):
        ev = ref[0, pl.ds(0, nsel, stride=2), :]
        od = ref[0, pl.ds(1, nsel, stride=2), :]
        if nsel == half:
            return jnp.concatenate([ev, od], axis=0)
        pad = jnp.zeros((half - nsel, LANES), F32)
        return jnp.concatenate([ev, pad, od, pad], axis=0)

    kc = permuted(kc_ref)
    vc = permuted(vc_ref)
    kc_sw = pltpu.roll(kc, half, 1)
    vc_sw = pltpu.roll(vc, half, 1)
    low = _iota((LANES, LANES), 1) < half

    n_i = _iota((tq, LANES), 1)
    odd = n_i >= half
    slot = jnp.where(odd, n_i - half, n_i)
    cmp_blk = jnp.where(odd, 2 * slot + 1, 2 * slot)
    qpos = qt * tq + _iota((tq, LANES), 0)
    dist = (qpos - (cmp_blk * L_CMP + (L_CMP - 1))).astype(F32)
    ok = (dist >= 0) & (slot < nsel)

    q = q_ref[0] * SCALE
    o_pairs = [jnp.zeros((tq, LANES), F32) for _ in range(4)]
    for kvh in range(2):
        imp = jnp.zeros((tq, LANES), F32)
        for g in range(4):
            h = kvh * 4 + g
            par, pair = h % 2, h // 2
            qb = q[:, pair * LANES:(pair + 1) * LANES].astype(BF16)
            z = _dot_nt(qb, _half_variant(kc, kc_sw, kvh, par, low))
            s = jnp.where(ok, z - _head_slope(h) * dist, NEG_BIG)
            e = jnp.exp(s - jnp.max(s, axis=1, keepdims=True))
            p = jnp.where(ok, e / jnp.sum(e, axis=1, keepdims=True), 0.0)
            imp = imp + p
            o_pairs[pair] = o_pairs[pair] + _dot(p.astype(BF16), _half_variant(vc, vc_sw, kvh, par, low))
        imp_sel = imp + pltpu.roll(imp, half, 1)
        blk = n_i
        cur = qpos >> 6
        forced = (blk == 0) | (blk == cur) | (blk == cur - 1)
        valid = (blk * L_SEL <= qpos) & (blk < nsel)
        score = jnp.where(valid, imp_sel + jnp.where(forced, FORCE_BONUS, 0.0), -jnp.inf)
        rank = jnp.zeros((tq, LANES), F32)
        for i in range(nsel):
            si = score[:, i:i + 1]
            beats = (si > score) | ((si == score) & (blk > i))
            rank = rank + jnp.where(beats, 1.0, 0.0)
        sel_ref[0, kvh] = jnp.where(valid & (rank < ntop), 1.0, 0.0)
    oc_ref[0] = jnp.concatenate(o_pairs, axis=1)


def _nsa_cmp_prompt(proj3, kcmp, vcmp):
    b, t, _ = proj3.shape
    ncmp = kcmp.shape[1]
    tq = min(t, 256)
    qb = COL["b_q"] // W_GRP
    return pl.pallas_call(
        functools.partial(_nsa_cmp_kernel, tq=tq, ncmp=ncmp),
        grid=(b, t // tq),
        in_specs=[pl.BlockSpec((1, tq, W_GRP), lambda bi, i: (bi, i, qb)),
                  pl.BlockSpec((1, ncmp, LANES), lambda bi, i: (bi, 0, 0)),
                  pl.BlockSpec((1, ncmp, LANES), lambda bi, i: (bi, 0, 0))],
        out_specs=[pl.BlockSpec((1, tq, W_GRP), lambda bi, i: (bi, i, 0)),
                   pl.BlockSpec((1, 2, tq, LANES), lambda bi, i: (bi, 0, i, 0))],
        out_shape=[jax.ShapeDtypeStruct((b, t, W_GRP), F32),
                   jax.ShapeDtypeStruct((b, 2, t, LANES), F32)],
        compiler_params=_params(("parallel", "parallel")),
        name="nsa_cmp_prompt",
    )(proj3, kcmp, vcmp)


def _nsa_att_kernel(q_ref, ks_ref, vs_ref, kw_ref, vw_ref, sel_ref, br_ref, oc_ref, o_ref,
                    ksv, vsv, kwv, vwv, *, tq, t):
    qt = pl.program_id(1)

    @pl.when(qt == 0)
    def _():
        low = _iota((t, LANES), 1) < LANES // 2
        for src, dst in ((ks_ref, ksv), (vs_ref, vsv), (kw_ref, kwv), (vw_ref, vwv)):
            x = src[0]
            xs = pltpu.roll(x, LANES // 2, 1)
            for kvh in range(2):
                for par in range(2):
                    dst[kvh * 2 + par] = _half_variant(x, xs, kvh, par, low)

    q = q_ref[0] * SCALE
    sig = _sigmoid(br_ref[0])
    oc = oc_ref[0]
    rows = _iota((tq, tq), 0)
    cols = _iota((tq, tq), 1)
    lane = _iota((tq, LANES), 1)
    blk_row = _iota((LANES, tq), 0)
    blk_col = _iota((LANES, tq), 1)
    init = (jnp.full((tq, 1), NEG_BIG, F32), jnp.zeros((tq, 1), F32), jnp.zeros((tq, LANES), F32))

    for pair in range(4):
        out_pair = jnp.zeros((tq, LANES), F32)
        qb = q[:, pair * LANES:(pair + 1) * LANES].astype(BF16)
        oc_pair = oc[:, pair * LANES:(pair + 1) * LANES]
        for par in range(2):
            h = pair * 2 + par
            kvh = h // 4
            vi = kvh * 2 + par
            slope = _head_slope(h)
            selk = sel_ref[0, kvh].astype(BF16)

            def tile(kt, st, kv, vv, selected, vi=vi, slope=slope, selk=selk, qb=qb):
                m, l, acc = st
                start = pl.multiple_of(kt * tq, tq)
                k = kv[vi, pl.ds(start, tq), :]
                v = vv[vi, pl.ds(start, tq), :]
                dist = ((qt - kt) * tq + rows - cols).astype(F32)
                s = _dot_nt(qb, k) - slope * dist
                if selected:
                    expand = jnp.where(blk_row == ((start + blk_col) >> 6), 1.0, 0.0).astype(BF16)
                    mask = (_dot(selk, expand) > 0.5) & (dist >= 0)
                else:
                    mask = (dist >= 0) & (dist < WINDOW)
                m_new = jnp.maximum(m, jnp.max(jnp.where(mask, s, NEG_BIG), axis=1, keepdims=True))
                p = jnp.where(mask, jnp.exp(s - m_new), 0.0)
                alpha = jnp.exp(m - m_new)
                l = alpha * l + jnp.sum(p, axis=1, keepdims=True)
                acc = alpha * acc + _dot(p.astype(BF16), v)
                return m_new, l, acc

            _, l_s, acc_s = lax.fori_loop(
                0, qt + 1, lambda i, st: tile(qt - i, st, ksv, vsv, True), init)
            n_win = jnp.minimum(qt, (WINDOW + tq - 1) // tq) + 1
            _, l_w, acc_w = lax.fori_loop(
                0, n_win, lambda i, st: tile(qt - i, st, kwv, vwv, False), init)
            g_c = sig[:, 3 * h:3 * h + 1]
            g_s = sig[:, 3 * h + 1:3 * h + 2]
            g_w = sig[:, 3 * h + 2:3 * h + 3]
            o_h = g_c * oc_pair + g_s * (acc_s / l_s) + g_w * (acc_w / l_w)
            in_half = (lane < HEAD_DIM) if par == 0 else (lane >= HEAD_DIM)
            out_pair = jnp.where(in_half, o_h, out_pair)
        o_ref[0, :, pair * LANES:(pair + 1) * LANES] = out_pair


def _nsa_att_prompt(proj3, sel, oc):
    b, t, _ = proj3.shape
    tq = min(t, 256)
    qb = COL["b_q"] // W_GRP
    kv_blocks = [COL[k] // LANES for k in ("b_ks", "b_vs", "b_kw", "b_vw")]
    br = COL["b_br"] // LANES
    in_specs = [pl.BlockSpec((1, tq, W_GRP), lambda bi, i: (bi, i, qb))]
    in_specs += [pl.BlockSpec((1, t, LANES), functools.partial(lambda bi, i, cb: (bi, 0, cb), cb=cb))
                 for cb in kv_blocks]
    in_specs += [pl.BlockSpec((1, 2, tq, LANES), lambda bi, i: (bi, 0, i, 0)),
                 pl.BlockSpec((1, tq, LANES), lambda bi, i: (bi, i, br)),
                 pl.BlockSpec((1, tq, W_GRP), lambda bi, i: (bi, i, 0))]
    return pl.pallas_call(
        functools.partial(_nsa_att_kernel, tq=tq, t=t),
        grid=(b, t // tq),
        in_specs=in_specs,
        out_specs=pl.BlockSpec((1, tq, W_GRP), lambda bi, i: (bi, i, 0)),
        out_shape=jax.ShapeDtypeStruct((b, t, W_GRP), F32),
        scratch_shapes=[pltpu.VMEM((4, t, LANES), BF16)] * 4,
        compiler_params=_params(("parallel", "arbitrary"), 48),
        name="nsa_att_prompt",
    )(proj3, proj3, proj3, proj3, proj3, sel, proj3, oc)


def _pool_kernel(*refs, tt, start, has_halo):
    if has_halo:
        u_ref, halo_ref, pre_ref, w_ref, sc_ref, o_ref, e_ref, d_ref = refs
    else:
        u_ref, pre_ref, w_ref, sc_ref, o_ref, e_ref, d_ref = refs
    ti = pl.program_id(1)
    e_ref[POOL_HALO:POOL_HALO + tt, :] = u_ref[0]
    if has_halo:
        @pl.when(ti == 0)
        def _():
            e_ref[0:POOL_HALO, :] = pre_ref[0]

        @pl.when(ti > 0)
        def _():
            e_ref[0:POOL_HALO, :] = halo_ref[0]
    else:
        e_ref[0:POOL_HALO, :] = pre_ref[0]

    ch = min(tt, 64)
    for c in range(tt // ch):
        r0 = c * ch
        pos = start + ti * tt + r0 + _iota((ch, LANES), 0)
        for g, win in enumerate(POOL_WINDOWS):
            lanes = slice(g * LANES, (g + 1) * LANES)
            cur = e_ref[POOL_HALO + r0:POOL_HALO + r0 + ch, lanes]
            total = cur
            for s in range(1, win):
                total = total + e_ref[POOL_HALO + r0 - s:POOL_HALO + r0 - s + ch, lanes]
            cnt = jnp.minimum(win, pos + 1).astype(F32)
            d_ref[r0:r0 + ch, lanes] = total / cnt - cur
    for g in range(len(POOL_WINDOWS)):
        lanes = slice(g * LANES, (g + 1) * LANES)
        y = _dot(d_ref[:, lanes].astype(BF16), w_ref[g])
        o_ref[0, :, lanes] = y * sc_ref[:, lanes]


def _pool(u_arr, cb, prefix, w_pool, pool_scale, layer, start):
    b, t, _ = u_arr.shape
    tt = min(t, 512)
    nt = t // tt
    has_halo = nt > 1
    in_specs = [pl.BlockSpec((1, tt, W_GRP), lambda bi, i: (bi, i, cb))]
    args = [u_arr]
    if has_halo:
        per = tt // POOL_HALO
        in_specs.append(pl.BlockSpec((1, POOL_HALO, W_GRP),
                                     lambda bi, i: (bi, jnp.maximum(i * per - 1, 0), cb)))
        args.append(u_arr)
    in_specs += [pl.BlockSpec((1, POOL_HALO, W_GRP), lambda bi, i: (bi, 0, 0)),
                 pl.BlockSpec((None, 4, LANES, LANES), lambda bi, i: (layer, 0, 0, 0)),
                 pl.BlockSpec((None, 1, W_GRP), lambda bi, i: (layer, 0, 0))]
    args += [prefix, w_pool, pool_scale]
    return pl.pallas_call(
        functools.partial(_pool_kernel, tt=tt, start=start, has_halo=has_halo),
        grid=(b, nt),
        in_specs=in_specs,
        out_specs=pl.BlockSpec((1, tt, W_GRP), lambda bi, i: (bi, i, 0)),
        out_shape=jax.ShapeDtypeStruct((b, t, W_GRP), F32),
        scratch_shapes=[pltpu.VMEM((POOL_HALO + tt, W_GRP), F32), pltpu.VMEM((tt, W_GRP), F32)],
        compiler_params=_params(("parallel", "arbitrary")),
        name="pool",
    )(*args)


def _conv_kernel(*refs, tt, has_halo):
    if has_halo:
        (a_ref, g_ref, ha_ref, hg_ref, pre_ref, cw_ref, cb_ref, lg_ref, lb_ref, pw_ref,
         o_ref, u_ref, e_ref, y_ref) = refs
    else:
        (a_ref, g_ref, pre_ref, cw_ref, cb_ref, lg_ref, lb_ref, pw_ref,
         o_ref, u_ref, e_ref, y_ref) = refs
    ti = pl.program_id(1)
    u = a_ref[0] * _sigmoid(g_ref[0])
    u_ref[0] = u
    e_ref[CONV_HALO:CONV_HALO + tt, :] = u
    if has_halo:
        @pl.when(ti == 0)
        def _():
            e_ref[0:CONV_HALO, :] = pre_ref[0]

        @pl.when(ti > 0)
        def _():
            e_ref[0:CONV_HALO, :] = ha_ref[0] * _sigmoid(hg_ref[0])
    else:
        e_ref[0:CONV_HALO, :] = pre_ref[0]

    ch = min(tt, 32)
    lead = CONV_HALO - (CONV_W - 1)
    for c in range(tt // ch):
        r0 = c * ch
        acc = jnp.broadcast_to(cb_ref[...], (ch, W_GRP))
        for j in range(CONV_W):
            acc = acc + cw_ref[j:j + 1, :] * e_ref[r0 + lead + j:r0 + lead + j + ch, :]
        y_ref[r0:r0 + ch, :] = acc
    y = y_ref[...]
    yc = y - jnp.mean(y, axis=-1, keepdims=True)
    yn = yc * lax.rsqrt(jnp.mean(yc * yc, axis=-1, keepdims=True) + LN_EPS)
    yn = yn * lg_ref[...] + lb_ref[...]
    o_ref[0] = _dot((yn * _sigmoid(yn)).astype(BF16), pw_ref[...])


def _conv(arr, cb_a, cb_g, prefix, conv_w, conv_b, ln_g, ln_b, w_pw, layer):
    b, t, _ = arr.shape
    tt = min(t, 512)
    nt = t // tt
    has_halo = nt > 1
    in_specs = [pl.BlockSpec((1, tt, W_GRP), lambda bi, i: (bi, i, cb_a)),
                pl.BlockSpec((1, tt, W_GRP), lambda bi, i: (bi, i, cb_g))]
    args = [arr, arr]
    if has_halo:
        per = tt // CONV_HALO
        in_specs += [pl.BlockSpec((1, CONV_HALO, W_GRP),
                                  lambda bi, i: (bi, jnp.maximum(i * per - 1, 0), cb_a)),
                     pl.BlockSpec((1, CONV_HALO, W_GRP),
                                  lambda bi, i: (bi, jnp.maximum(i * per - 1, 0), cb_g))]
        args += [arr, arr]
    vec = lambda: pl.BlockSpec((None, 1, W_GRP), lambda bi, i: (layer, 0, 0))
    in_specs += [pl.BlockSpec((1, CONV_HALO, W_GRP), lambda bi, i: (bi, 0, 0)),
                 pl.BlockSpec((None, CONV_HALO, W_GRP), lambda bi, i: (layer, 0, 0)),
                 vec(), vec(), vec(),
                 pl.BlockSpec((None, W_GRP, W_GRP), lambda bi, i: (layer, 0, 0))]
    args += [prefix, conv_w, conv_b, ln_g, ln_b, w_pw]
    return pl.pallas_call(
        functools.partial(_conv_kernel, tt=tt, has_halo=has_halo),
        grid=(b, nt),
        in_specs=in_specs,
        out_specs=[pl.BlockSpec((1, tt, W_GRP), lambda bi, i: (bi, i, 0))] * 2,
        out_shape=[jax.ShapeDtypeStruct((b, t, W_GRP), F32)] * 2,
        scratch_shapes=[pltpu.VMEM((CONV_HALO + tt, W_GRP), F32), pltpu.VMEM((tt, W_GRP), F32)],
        compiler_params=_params(("parallel", "arbitrary")),
        name="conv",
    )(*args)


SB_ROWS = 32
PAGES_PER_STEP = 8


def _sb_sample_kernel(pt_ref, q_ref, kn_ref, vn_ref, *rest, npg, page):
    del pt_ref
    krefs, vrefs = rest[:npg], rest[npg:2 * npg]
    o_ref, carry_ref, acc_ref = rest[2 * npg:]
    step = pl.program_id(1)
    qbd = q_ref[0].astype(BF16)
    upper = _upper_ones(page)

    def absorb(k, v, mask):
        z = _dot_nt(qbd, k.astype(BF16))
        lk = _log_keep(z)
        if mask is not None:
            lk = jnp.where(mask, lk, 0.0)
        between = _split_dot(lk, upper) + carry_ref[...]
        a = jnp.exp(z + lk + between)
        if mask is not None:
            a = jnp.where(mask, a, 0.0)
        acc_ref[...] += _dot(a.astype(BF16), v.astype(BF16))
        carry_ref[...] += jnp.sum(lk, axis=1, keepdims=True)

    @pl.when(step == 0)
    def _():
        carry_ref[...] = jnp.zeros_like(carry_ref)
        acc_ref[...] = jnp.zeros_like(acc_ref)
        pad = jnp.zeros((page - kn_ref.shape[1], W_GRP), F32)
        r = _iota((SB_ROWS, page), 0)
        j = _iota((SB_ROWS, page), 1)
        absorb(jnp.concatenate([kn_ref[0], pad], axis=0),
               jnp.concatenate([vn_ref[0], pad], axis=0), j < (r >> 3))

    for g in range(npg):
        absorb(krefs[g][0, 0], vrefs[g][0, 0], None)

    @pl.when(step == pl.num_programs(1) - 1)
    def _():
        acc = acc_ref[...]
        r = _iota((SB_ROWS, HEAD_DIM), 0)
        out = jnp.zeros((SB_ROWS, HEAD_DIM), F32)
        for h in range(W_GRP // HEAD_DIM):
            out = jnp.where((r & 7) == h, acc[:, h * HEAD_DIM:(h + 1) * HEAD_DIM], out)
        o_ref[0] = out


def _sb_sample(page_table, qbd, k_new, v_new, cache_k, cache_v, layer):
    db, n_pages = page_table.shape
    page = cache_k.shape[2]
    npg = min(PAGES_PER_STEP, n_pages)
    steps = n_pages // npg

    def page_spec(g):
        def index(bi, s, pt):
            return (layer, pt[bi * n_pages + (n_pages - 1 - (s * npg + g))], 0, 0)
        return pl.BlockSpec((1, 1, page, W_GRP), index)

    small = lambda r: pl.BlockSpec((1, r, W_GRP), lambda bi, s, pt: (bi, 0, 0))
    grid_spec = pltpu.PrefetchScalarGridSpec(
        num_scalar_prefetch=1,
        grid=(db, steps),
        in_specs=[small(SB_ROWS), small(k_new.shape[1]), small(v_new.shape[1])]
        + [page_spec(g) for g in range(npg)] * 2,
        out_specs=pl.BlockSpec((1, SB_ROWS, HEAD_DIM), lambda bi, s, pt: (bi, 0, 0)),
        scratch_shapes=[pltpu.VMEM((SB_ROWS, 1), F32), pltpu.VMEM((SB_ROWS, W_GRP), F32)],
    )
    return pl.pallas_call(
        functools.partial(_sb_sample_kernel, npg=npg, page=page),
        grid_spec=grid_spec,
        out_shape=jax.ShapeDtypeStruct((db, SB_ROWS, HEAD_DIM), F32),
        compiler_params=_params(("parallel", "arbitrary")),
        name="sb_sample",
    )(page_table.reshape(-1), qbd, k_new, v_new, *([cache_k] * npg), *([cache_v] * npg))


def _row_fields(shape):
    r = _iota(shape, 0)
    qi = (r >> 1) & 3
    kvh = r & 1
    h = kvh * 4 + (r >> 3)
    slope = jnp.zeros(shape, F32)
    for hh in range(8):
        slope = jnp.where(h == hh, _head_slope(hh), slope)
    return qi, kvh, slope


def _pick_half(x, kvh64):
    return jnp.where(kvh64 == 0, x[:, :HEAD_DIM], x[:, HEAD_DIM:])


def _nsa_small_kernel(pt_ref, q_ref, kall_ref, vall_ref, kwc_ref, vwc_ref, kwn_ref, vwn_ref,
                      oc_ref, ow_ref, sel_ref, kg_ref, vg_ref, *, n_pages, past):
    b = pl.program_id(0)

    def gather(p, carry):
        idx = pt_ref[b * n_pages + p]
        kg_ref[pl.ds(p, 1), :] = kall_ref[pl.ds(idx, 1), :]
        vg_ref[pl.ds(p, 1), :] = vall_ref[pl.ds(idx, 1), :]
        return carry

    lax.fori_loop(0, n_pages, gather, 0)

    q = q_ref[0].astype(BF16)
    rows = q_ref.shape[1]
    _, kvh64, _ = _row_fields((rows, HEAD_DIM))

    qi, _, slope = _row_fields((rows, n_pages))
    pidx = _iota((rows, n_pages), 1)
    qpos = past + qi
    per_page = kg_ref.shape[1] // LANES
    scores, oks = [], []
    for n4 in range(per_page):
        z = _dot_nt(q, kg_ref[:, n4 * LANES:(n4 + 1) * LANES].astype(BF16))
        dist = (qpos - ((pidx * per_page + n4) * L_CMP + (L_CMP - 1))).astype(F32)
        ok = dist >= 0
        oks.append(ok)
        scores.append(jnp.where(ok, z - slope * dist, NEG_BIG))
    m = functools.reduce(jnp.maximum, [jnp.max(s, axis=1, keepdims=True) for s in scores])
    es = [jnp.exp(s - m) for s in scores]
    tot = functools.reduce(jnp.add, [jnp.sum(e, axis=1, keepdims=True) for e in es])
    ps = [jnp.where(ok, e / tot, 0.0) for ok, e in zip(oks, es)]
    o_c = functools.reduce(jnp.add, [
        _dot(p.astype(BF16), vg_ref[:, n4 * LANES:(n4 + 1) * LANES].astype(BF16))
        for n4, p in enumerate(ps)])
    oc_ref[0] = _pick_half(o_c, kvh64)

    def group_sum(x):
        return x[0:8] + x[8:16] + x[16:24] + x[24:32]

    lane = _iota((8, n_pages), 1)
    s_even = group_sum(ps[0] + ps[1]) + jnp.where(lane == 0, FORCE_BONUS, 0.0)
    s_odd = group_sum(ps[2] + ps[3]) + jnp.where(lane == n_pages - 1, FORCE_BONUS, 0.0)
    score = jnp.concatenate([s_even, s_odd], axis=1)
    lane2 = _iota((8, 2 * n_pages), 1)
    blk = jnp.where(lane2 < n_pages, 2 * lane2, 2 * (lane2 - n_pages) + 1)
    sel = jnp.zeros((8, 2 * n_pages), F32)
    for _ in range(N_SEL - 1):
        best = jnp.max(score, axis=1, keepdims=True)
        first = jnp.min(jnp.where(score == best, blk, 2 ** 30), axis=1, keepdims=True)
        hit = (blk == first) & (best > -jnp.inf)
        sel = jnp.where(hit, 1.0, sel)
        score = jnp.where(blk == first, -jnp.inf, score)
    sel_ref[0] = sel

    wbuf = kwc_ref.shape[2]
    n_new = kwn_ref.shape[1]
    padw = jnp.zeros((LANES - n_new, LANES), F32)
    kw = jnp.concatenate([kwc_ref[0, 0], kwn_ref[0], padw], axis=0)
    vw = jnp.concatenate([vwc_ref[0, 0], vwn_ref[0], padw], axis=0)
    nk = wbuf + LANES
    qi, _, slope = _row_fields((rows, nk))
    c = _iota((rows, nk), 1)
    dist = jnp.where(c < wbuf, wbuf + qi - c, qi - (c - wbuf))
    ok = (dist >= 0) & (dist < WINDOW)
    s = jnp.where(ok, _dot_nt(q, kw.astype(BF16)) - slope * dist.astype(F32), NEG_BIG)
    e = jnp.exp(s - jnp.max(s, axis=1, keepdims=True))
    p = jnp.where(ok, e / jnp.sum(e, axis=1, keepdims=True), 0.0)
    ow_ref[0] = _pick_half(_dot(p.astype(BF16), vw.astype(BF16)), kvh64)


def _nsa_small(page_table, q_rows, kall, vall, win_k, win_v, kw_new, vw_new, layer, past):
    db, n_pages = page_table.shape
    rows = q_rows.shape[1]
    wbuf = win_k.shape[2]
    whole = lambda a: pl.BlockSpec(a.shape, lambda bi, pt: (0,) * a.ndim)
    per_b = lambda a: pl.BlockSpec((1,) + a.shape[1:], lambda bi, pt: (bi,) + (0,) * (a.ndim - 1))
    win = pl.BlockSpec((1, 1, wbuf, LANES), lambda bi, pt: (layer, bi, 0, 0))
    grid_spec = pltpu.PrefetchScalarGridSpec(
        num_scalar_prefetch=1,
        grid=(db,),
        in_specs=[per_b(q_rows), whole(kall), whole(vall), win, win, per_b(kw_new), per_b(vw_new)],
        out_specs=[pl.BlockSpec((1, rows, HEAD_DIM), lambda bi, pt: (bi, 0, 0)),
                   pl.BlockSpec((1, rows, HEAD_DIM), lambda bi, pt: (bi, 0, 0)),
                   pl.BlockSpec((1, 8, 2 * n_pages), lambda bi, pt: (bi, 0, 0))],
        scratch_shapes=[pltpu.VMEM((n_pages, kall.shape[1]), F32)] * 2,
    )
    return pl.pallas_call(
        functools.partial(_nsa_small_kernel, n_pages=n_pages, past=past),
        grid_spec=grid_spec,
        out_shape=[jax.ShapeDtypeStruct((db, rows, HEAD_DIM), F32),
                   jax.ShapeDtypeStruct((db, rows, HEAD_DIM), F32),
                   jax.ShapeDtypeStruct((db, 8, 2 * n_pages), F32)],
        compiler_params=_params(("arbitrary",)),
        name="nsa_small_sample",
    )(page_table.reshape(-1), q_rows, kall, vall, win_k, win_v, kw_new, vw_new)


def _nsa_sel_kernel(pt_ref, q_ref, sel_ref, kn_ref, vn_ref, oc_ref, ow_ref, br_ref, *rest,
                    npg, n_pages, page, past):
    del pt_ref
    krefs, vrefs = rest[:npg], rest[npg:2 * npg]
    o_ref, m_ref, l_ref, acc_ref = rest[2 * npg:]
    step = pl.program_id(1)
    q = q_ref[0].astype(BF16)
    rows = q_ref.shape[1]
    qi, _, slope = _row_fields((rows, page))
    lane = _iota((rows, page), 1)
    qpos = past + qi

    def absorb(k, v, kpos0, mask):
        dist = (qpos - (kpos0 + lane)).astype(F32)
        s = _dot_nt(q, k.astype(BF16)) - slope * dist
        if mask is None:
            mask = dist >= 0
        m_old = m_ref[...]
        m_new = jnp.maximum(m_old, jnp.max(jnp.where(mask, s, NEG_BIG), axis=1, keepdims=True))
        p = jnp.where(mask, jnp.exp(s - m_new), 0.0)
        alpha = jnp.exp(m_old - m_new)
        l_ref[...] = alpha * l_ref[...] + jnp.sum(p, axis=1, keepdims=True)
        acc_ref[...] = alpha * acc_ref[...] + _dot(p.astype(BF16), v.astype(BF16))
        m_ref[...] = m_new

    @pl.when(step == 0)
    def _():
        m_ref[...] = jnp.full_like(m_ref, NEG_BIG)
        l_ref[...] = jnp.zeros_like(l_ref)
        acc_ref[...] = jnp.zeros_like(acc_ref)
        pad = jnp.zeros((page - kn_ref.shape[1], LANES), F32)
        absorb(jnp.concatenate([kn_ref[0], pad], axis=0),
               jnp.concatenate([vn_ref[0], pad], axis=0), past, None)

    selm = jnp.concatenate([sel_ref[0]] * (rows // 8), axis=0)
    pl_lane = _iota((rows, n_pages), 1)
    per_blk = page // L_SEL
    for g in range(npg):
        pg = n_pages - 1 - (step * npg + g)
        mask = jnp.zeros((rows, page), jnp.bool_)
        for j in range(per_blk):
            cj = jnp.sum(jnp.where(pl_lane == pg, selm[:, j * n_pages:(j + 1) * n_pages], 0.0),
                         axis=1, keepdims=True)
            mask = mask | ((cj > 0.5) & (lane >= j * L_SEL) & (lane < (j + 1) * L_SEL))
        absorb(krefs[g][0, 0], vrefs[g][0, 0], pg * page, mask)

    @pl.when(step == pl.num_programs(1) - 1)
    def _():
        _, kvh64, _ = _row_fields((rows, HEAD_DIM))
        o_s = _pick_half(acc_ref[...] / l_ref[...], kvh64)
        sig = _sigmoid(br_ref[0])
        o_ref[0] = sig[:, 0:1] * oc_ref[0] + sig[:, 1:2] * o_s + sig[:, 2:3] * ow_ref[0]


def _nsa_sel_sample(page_table, q_rows, sel, k_new, v_new, oc, ow, br_rows, cache_k, cache_v,
                    layer, past):
    db, n_pages = page_table.shape
    rows = q_rows.shape[1]
    page = cache_k.shape[2]
    npg = min(PAGES_PER_STEP, n_pages)
    steps = n_pages // npg

    def page_spec(g):
        def index(bi, s, pt):
            return (layer, pt[bi * n_pages + (n_pages - 1 - (s * npg + g))], 0, 0)
        return pl.BlockSpec((1, 1, page, LANES), index)

    per_b = lambda a: pl.BlockSpec((1,) + a.shape[1:], lambda bi, s, pt: (bi,) + (0,) * (a.ndim - 1))
    grid_spec = pltpu.PrefetchScalarGridSpec(
        num_scalar_prefetch=1,
        grid=(db, steps),
        in_specs=[per_b(a) for a in (q_rows, sel, k_new, v_new, oc, ow, br_rows)]
        + [page_spec(g) for g in range(npg)] * 2,
        out_specs=pl.BlockSpec((1, rows, HEAD_DIM), lambda bi, s, pt: (bi, 0, 0)),
        scratch_shapes=[pltpu.VMEM((rows, 1), F32), pltpu.VMEM((rows, 1), F32),
                        pltpu.VMEM((rows, LANES), F32)],
    )
    return pl.pallas_call(
        functools.partial(_nsa_sel_kernel, npg=npg, n_pages=n_pages, page=page, past=past),
        grid_spec=grid_spec,
        out_shape=jax.ShapeDtypeStruct((db, rows, HEAD_DIM), F32),
        compiler_params=_params(("parallel", "arbitrary")),
        name="nsa_sel_sample",
    )(page_table.reshape(-1), q_rows, sel, k_new, v_new, oc, ow, br_rows,
      *([cache_k] * npg), *([cache_v] * npg))


def _pack_w_in(w_in):
    depth, d, _ = w_in.shape
    pieces, src = [], 0
    placed = {}
    for name, width in _SRC:
        placed[name] = (src, width)
        src += width
    order = sorted(COL, key=COL.get)
    pos = 0
    for name in order:
        assert COL[name] == pos
        s, width = placed[name]
        pieces.append(w_in[:, :, s:s + width])
        pos += width
    pieces.append(jnp.zeros((depth, d, N_PACK - pos), w_in.dtype))
    return jnp.concatenate(pieces, axis=2).astype(BF16)


def _expand_cmp_weight(w):
    depth, l, d, e = w.shape
    eye = jnp.eye(2, dtype=w.dtype)
    return jnp.einsum("zlde,hg->zlhdge", w, eye).reshape(depth, l * 2 * d, 2 * e).astype(BF16)


def _sample_rows(x, n_heads_last):
    db, tq = x.shape[:2]
    x = x.reshape(db, tq, 2, 4, n_heads_last)
    return x.transpose(0, 3, 1, 2, 4).reshape(db, 4 * tq * 2, n_heads_last)


def kernel(x_prompt, x_sample, cache_sb_k, cache_sb_v, cache_cmp_k, cache_cmp_v, cache_sel_k, cache_sel_v, cache_win_k, cache_win_v, state_pool, state_conv, page_table, norm_g, w_in, w_cmp_k, w_cmp_v, w_pool, pool_scale, conv_w, conv_b, ln_g, ln_b, w_pw, w_out, final_g):
    bp, seq, d_model = x_prompt.shape
    db, tq_s, _ = x_sample.shape
    depth, n_pool, page = cache_sb_k.shape[:3]
    assert tq_s == 4 and page == LANES and cache_win_k.shape[2] == WINDOW and seq % 256 == 0

    w = _prepare_weights(norm_g, w_in, w_cmp_k, w_cmp_v, w_pool, pool_scale, conv_w, conv_b,
                         ln_g, ln_b, w_pw, w_out, final_g)
    sbk = cache_sb_k.reshape(depth, n_pool, page, W_GRP)
    sbv = cache_sb_v.reshape(depth, n_pool, page, W_GRP)
    selk = cache_sel_k.reshape(depth, n_pool, page, LANES)
    selv = cache_sel_v.reshape(depth, n_pool, page, LANES)
    cmpk = cache_cmp_k.reshape(depth * n_pool * page // L_CMP, L_CMP * LANES)
    cmpv = cache_cmp_v.reshape(depth * n_pool * page // L_CMP, L_CMP * LANES)

    hp = x_prompt.reshape(bp * seq, d_model)
    hs = x_sample.reshape(db * tq_s, d_model)
    p_states, s_states = [], []
    for l in range(depth):
        last = l == depth - 1
        hp, st = _prompt_layer(hp, bp, seq, w, l, last)
        p_states.append(st)
        hs, st = _sample_layer(hs, db, tq_s, w, l, last, page_table, sbk, sbv, cmpk, cmpv, selk, selv,
                               cache_win_k, cache_win_v, state_pool, state_conv)
        s_states.append(st)

    stacked = lambda states, j: jnp.stack([st[j] for st in states], axis=0)
    return (hp.reshape(bp, seq, d_model), hs.reshape(db, tq_s, d_model),
            *[stacked(p_states, j) for j in range(10)],
            *[stacked(s_states, j) for j in range(10)])


def _prepare_weights(norm_g, w_in, w_cmp_k, w_cmp_v, w_pool, pool_scale, conv_w, conv_b, ln_g, ln_b,
                     w_pw, w_out, final_g):
    depth = w_in.shape[0]
    vec3 = lambda a: a.reshape(depth, 1, -1)
    return dict(
        w_pack=_pack_w_in(w_in), w_out=w_out.astype(BF16),
        wk_exp=_expand_cmp_weight(w_cmp_k), wv_exp=_expand_cmp_weight(w_cmp_v),
        w_pool=w_pool.astype(BF16), w_pw=w_pw.astype(BF16),
        conv_w=jnp.pad(conv_w, ((0, 0), (0, CONV_HALO - CONV_W), (0, 0))),
        norm_g=vec3(norm_g), pool_scale=vec3(pool_scale), conv_b=vec3(conv_b),
        ln_g=vec3(ln_g), ln_b=vec3(ln_b), final_g=final_g.reshape(1, -1))


def _prompt_layer(hp, bp, seq, w, l, last):
    proj = _inproj(hp, w["norm_g"], w["w_pack"], l)
    proj3 = proj.reshape(bp, seq, N_PACK)
    col = lambda name, width: proj3[:, :, COL[name]:COL[name] + width]
    kc, vc = col("b_kc", LANES), col("b_vc", LANES)
    o_a = _sb_prompt(proj3)
    n_blocks = bp * seq // L_CMP
    kcmp, vcmp = _compress(kc.reshape(n_blocks, L_CMP * LANES), vc.reshape(n_blocks, L_CMP * LANES),
                           w["wk_exp"], w["wv_exp"], l, 0, n_blocks)
    n_cmp = seq // L_CMP
    o_cmp, sel = _nsa_cmp_prompt(proj3, kcmp.reshape(bp, n_cmp, LANES), vcmp.reshape(bp, n_cmp, LANES))
    o_b = _nsa_att_prompt(proj3, sel, o_cmp)
    zero_pool = jnp.zeros((bp, POOL_HALO, W_GRP), F32)
    zero_conv = jnp.zeros((bp, CONV_HALO, W_GRP), F32)
    o_c = _pool(proj3, COL["c_u"] // W_GRP, zero_pool, w["w_pool"], w["pool_scale"], l, 0)
    o_d, u_conv = _conv(proj3, COL["d_a"] // W_GRP, COL["d_gl"] // W_GRP, zero_conv,
                        w["conv_w"], w["conv_b"], w["ln_g"], w["ln_b"], w["w_pw"], l)
    flat = lambda a: a.reshape(bp * seq, W_GRP)
    hp_new = _outproj([flat(o_a), flat(o_b), flat(o_c), flat(o_d)], proj, hp, w["w_out"], l,
                      w["final_g"] if last else None)
    keep = min(WINDOW, seq)
    heads = lambda a, n: a.reshape(bp, -1, n, HEAD_DIM)
    states = (
        heads(col("a_k", W_GRP), 8), heads(col("a_v", W_GRP), 8),
        heads(kc, 2), heads(vc, 2),
        heads(col("b_ks", LANES), 2), heads(col("b_vs", LANES), 2),
        heads(col("b_kw", LANES)[:, seq - keep:], 2), heads(col("b_vw", LANES)[:, seq - keep:], 2),
        col("c_u", W_GRP)[:, seq - (POOL_HALO - 1):], u_conv[:, seq - (CONV_W - 1):])
    return hp_new, states


def _sample_layer(hs, db, tq_s, w, l, last, page_table, sbk, sbv, cmpk, cmpv, selk, selv,
                  cache_win_k, cache_win_v, state_pool, state_conv):
    depth, n_pool, page = sbk.shape[:3]
    n_pages = page_table.shape[1]
    past = n_pages * page
    wbuf = cache_win_k.shape[2]
    blocks_per_layer = n_pool * page // L_CMP
    projs = _inproj(hs, w["norm_g"], w["w_pack"], l)
    projs3 = projs.reshape(db, tq_s, N_PACK)
    scol = lambda name, width: projs3[:, :, COL[name]:COL[name] + width]
    pad8 = lambda a: jnp.pad(a, ((0, 0), (0, 8 - tq_s), (0, 0)))
    head_of_col = jnp.arange(W_GRP) // HEAD_DIM
    sb_mask = (head_of_col[None, :] == jnp.arange(8)[:, None]).astype(F32)
    a_q = scol("a_q", W_GRP) * SCALE
    qbd = (a_q[:, :, None, :] * sb_mask[None, None]).reshape(db, tq_s * 8, W_GRP)
    o_a = _sb_sample(page_table, qbd, pad8(scol("a_k", W_GRP)), pad8(scol("a_v", W_GRP)), sbk, sbv, l)
    o_a = o_a.reshape(db * tq_s, W_GRP)
    kall, vall = _compress(cmpk, cmpv, w["wk_exp"], w["wv_exp"], l, l * blocks_per_layer, blocks_per_layer)
    per_page = page // L_CMP
    kall = kall.reshape(n_pool, per_page * LANES)
    vall = vall.reshape(n_pool, per_page * LANES)
    b_q = _sample_rows(scol("b_q", W_GRP) * SCALE, HEAD_DIM)
    kvh_of_row = jnp.arange(b_q.shape[1]) & 1
    q_rows = (b_q[:, :, None, :] * jnp.eye(2, dtype=F32)[kvh_of_row][None, :, :, None]).reshape(db, -1, LANES)
    br_rows = jnp.pad(_sample_rows(scol("b_br", 24), 3), ((0, 0), (0, 0), (0, LANES - 3)))
    kw_new, vw_new = scol("b_kw", LANES), scol("b_vw", LANES)
    wink = cache_win_k.reshape(depth, db, wbuf, LANES)
    winv = cache_win_v.reshape(depth, db, wbuf, LANES)
    o_cmp, o_win, sel = _nsa_small(page_table, q_rows, kall, vall, wink, winv,
                                   pad8(kw_new), pad8(vw_new), l, past)
    o_b_rows = _nsa_sel_sample(page_table, q_rows, sel, pad8(scol("b_ks", LANES)),
                               pad8(scol("b_vs", LANES)), o_cmp, o_win, br_rows, selk, selv, l, past)
    o_b = o_b_rows.reshape(db, 4, tq_s, 2, HEAD_DIM).transpose(0, 2, 3, 1, 4).reshape(db * tq_s, W_GRP)
    pool_pre = jnp.pad(state_pool[l], ((0, 0), (POOL_HALO - state_pool.shape[2], 0), (0, 0)))
    o_c = _pool(pad8(scol("c_u", W_GRP)), 0, pool_pre, w["w_pool"], w["pool_scale"], l, past)
    o_c = o_c[:, :tq_s].reshape(db * tq_s, W_GRP)
    conv_pre = jnp.pad(state_conv[l], ((0, 0), (CONV_HALO - state_conv.shape[2], 0), (0, 0)))
    o_d, u_conv = _conv(pad8(scol("d_a", 2 * W_GRP)), 0, 1, conv_pre,
                        w["conv_w"], w["conv_b"], w["ln_g"], w["ln_b"], w["w_pw"], l)
    o_d = o_d[:, :tq_s].reshape(db * tq_s, W_GRP)
    hs_new = _outproj([o_a, o_b, o_c, o_d], projs, hs, w["w_out"], l, w["final_g"] if last else None)
    sheads = lambda a, n: a.reshape(db, -1, n, HEAD_DIM)
    states = (
        sheads(scol("a_k", W_GRP), 8), sheads(scol("a_v", W_GRP), 8),
        sheads(scol("b_kc", LANES), 2), sheads(scol("b_vc", LANES), 2),
        sheads(scol("b_ks", LANES), 2), sheads(scol("b_vs", LANES), 2),
        jnp.concatenate([cache_win_k[l][:, tq_s:], sheads(kw_new, 2)], axis=1),
        jnp.concatenate([cache_win_v[l][:, tq_s:], sheads(vw_new, 2)], axis=1),
        jnp.concatenate([state_pool[l][:, tq_s:], scol("c_u", W_GRP)], axis=1),
        jnp.concatenate([state_conv[l][:, tq_s:], u_conv[:, :tq_s]], axis=1))
    return hs_new, states
```

```python
import functools

import jax
import jax.numpy as jnp
from jax import lax
from jax.experimental import pallas as pl
from jax.experimental.pallas import tpu as pltpu

F32 = jnp.float32
BF16 = jnp.bfloat16

HEAD_DIM = 64
LANES = 128
W_GRP = 512
L_CMP = 32
L_SEL = 64
N_SEL = 16
WINDOW = 512
FORCE_BONUS = 1000.0
POOL_WINDOWS = (2, 4, 8, 16)
POOL_HALO = 16
CONV_W = 31
CONV_HALO = 32
RMS_EPS = 1e-6
LN_EPS = 1e-5
NEG_BIG = -1e30
SCALE = HEAD_DIM ** -0.5
SB_DEAD_CARRY = -104.0

COL = dict(a_q=0, a_k=512, a_v=1024, a_g=1536, b_q=2048, b_g=2560, c_u=3072, c_g=3584,
           d_a=4096, d_gl=4608, d_g=5120, b_kc=5632, b_vc=5760, b_ks=5888, b_vs=6016,
           b_kw=6144, b_vw=6272, b_br=6400)
N_PACK = 6656
_SRC = (("a_q", 512), ("a_k", 512), ("a_v", 512), ("a_g", 512), ("b_q", 512), ("b_kc", 128),
        ("b_vc", 128), ("b_ks", 128), ("b_vs", 128), ("b_kw", 128), ("b_vw", 128), ("b_g", 512),
        ("b_br", 24), ("c_u", 512), ("c_g", 512), ("d_a", 512), ("d_gl", 512), ("d_g", 512))


def _params(sem, vmem_mb=None):
    kw = dict(dimension_semantics=sem)
    if vmem_mb is not None:
        kw["vmem_limit_bytes"] = vmem_mb * 2 ** 20
    return pltpu.CompilerParams(**kw)


def _sigmoid(x):
    return 1.0 / (1.0 + jnp.exp(-x))


def _iota(shape, axis):
    return lax.broadcasted_iota(jnp.int32, shape, axis)


def _dot_nt(a, b):
    return lax.dot_general(a, b, (((1,), (1,)), ((), ())), preferred_element_type=F32)


def _dot(a, b):
    return jnp.dot(a, b, preferred_element_type=F32)


def _split_dot(x, m):
    hi = x.astype(BF16)
    lo = (x - hi.astype(F32)).astype(BF16)
    return _dot(hi, m) + _dot(lo, m)


def _log_keep(z):
    return jnp.minimum(-z, 0.0) - jnp.log(1.0 + jnp.exp(-jnp.abs(z)))


def _upper_ones(n):
    return jnp.where(_iota((n, n), 0) > _iota((n, n), 1), 1.0, 0.0).astype(BF16)


def _inproj_kernel(x_ref, g_ref, w_ref, o_ref, hn_ref):
    @pl.when(pl.program_id(1) == 0)
    def _():
        x = x_ref[...]
        ms = jnp.mean(x * x, axis=-1, keepdims=True)
        hn_ref[...] = ((x * lax.rsqrt(ms + RMS_EPS)) * g_ref[...]).astype(BF16)

    o_ref[...] = _dot(hn_ref[...], w_ref[...])


def _inproj(x, norm_g, w_pack, layer):
    rows, d = x.shape
    n = w_pack.shape[2]
    tm = min(rows, 1024)
    tn = 512
    return pl.pallas_call(
        _inproj_kernel,
        grid=(rows // tm, n // tn),
        in_specs=[pl.BlockSpec((tm, d), lambda i, j: (i, 0)),
                  pl.BlockSpec((None, 1, d), lambda i, j: (layer, 0, 0)),
                  pl.BlockSpec((None, d, tn), lambda i, j: (layer, 0, j))],
        out_specs=pl.BlockSpec((tm, tn), lambda i, j: (i, j)),
        out_shape=jax.ShapeDtypeStruct((rows, n), F32),
        scratch_shapes=[pltpu.VMEM((tm, d), BF16)],
        compiler_params=_params(("parallel", "arbitrary"), 48),
        name="inproj",
    )(x, norm_g, w_pack)


def _outproj_kernel(oa, ob, oc, od, ga, gb, gc, gd, h_ref, w_ref, *rest, final):
    if final:
        fg_ref, o_ref = rest
    else:
        (o_ref,) = rest
    acc = h_ref[...]
    for i, (o, g) in enumerate(((oa, ga), (ob, gb), (oc, gc), (od, gd))):
        gv = g[...]
        mix = (o[...] * (gv * _sigmoid(gv))).astype(BF16)
        acc = acc + _dot(mix, w_ref[i * W_GRP:(i + 1) * W_GRP, :])
    if final:
        ms = jnp.mean(acc * acc, axis=-1, keepdims=True)
        acc = (acc * lax.rsqrt(ms + RMS_EPS)) * fg_ref[...]
    o_ref[...] = acc


def _outproj(outs, proj, h, w_out, layer, final_g=None):
    rows, d = h.shape
    tm = min(rows, 256)
    final = final_g is not None
    gate_blocks = [COL[k] // W_GRP for k in ("a_g", "b_g", "c_g", "d_g")]
    in_specs = [pl.BlockSpec((tm, W_GRP), lambda i: (i, 0)) for _ in range(4)]
    in_specs += [pl.BlockSpec((tm, W_GRP), functools.partial(lambda i, cb: (i, cb), cb=cb))
                 for cb in gate_blocks]
    in_specs += [pl.BlockSpec((tm, d), lambda i: (i, 0)),
                 pl.BlockSpec((None, 4 * W_GRP, d), lambda i: (layer, 0, 0))]
    args = list(outs) + [proj] * 4 + [h, w_out]
    if final:
        in_specs.append(pl.BlockSpec((1, d), lambda i: (0, 0)))
        args.append(final_g)
    return pl.pallas_call(
        functools.partial(_outproj_kernel, final=final),
        grid=(rows // tm,),
        in_specs=in_specs,
        out_specs=pl.BlockSpec((tm, d), lambda i: (i, 0)),
        out_shape=jax.ShapeDtypeStruct((rows, d), F32),
        compiler_params=_params(("parallel",), 48),
        name="outproj",
    )(*args)


def _sb_prompt_kernel(q_ref, k_ref, v_ref, o_ref, *, tq):
    qt = pl.program_id(2)
    lane = _iota((tq, LANES), 1)
    row = _iota((tq, tq), 0)
    col = _iota((tq, tq), 1)
    before = col < row
    upper = _upper_ones(tq)
    q = q_ref[0] * SCALE
    qms = [jnp.where((lane >= hh * HEAD_DIM) & (lane < (hh + 1) * HEAD_DIM), q, 0.0).astype(BF16)
           for hh in range(2)]

    def tile(kt, state, diag):
        start = pl.multiple_of(kt * tq, tq)
        k = k_ref[0, pl.ds(start, tq), :].astype(BF16)
        v = v_ref[0, pl.ds(start, tq), :].astype(BF16)
        new = []
        for qm, (carry, acc) in zip(qms, state):
            z = _dot_nt(qm, k)
            lk = _log_keep(z)
            if diag:
                lk = jnp.where(before, lk, 0.0)
            between = _split_dot(lk, upper) + carry
            a = jnp.exp(z + lk + between)
            if diag:
                a = jnp.where(before, a, 0.0)
            new.append((carry + jnp.sum(lk, axis=1, keepdims=True), acc + _dot(a.astype(BF16), v)))
        return tuple(new)

    def any_live(state):
        return jnp.maximum(jnp.max(state[0][0]), jnp.max(state[1][0])) > SB_DEAD_CARRY

    zero = (jnp.zeros((tq, 1), F32), jnp.zeros((tq, LANES), F32))
    state = tile(qt, (zero, zero), True)

    def cond(c):
        return (c[0] < qt) & c[1]

    def body(c):
        st = tile(qt - 1 - c[0], c[2], False)
        return c[0] + 1, any_live(st), st

    _, _, state = lax.while_loop(cond, body, (jnp.int32(0), any_live(state), state))
    o_ref[0] = jnp.where(lane < HEAD_DIM, state[0][1], state[1][1])


def _sb_prompt(proj3):
    b, t, _ = proj3.shape
    tq = min(t, 256)
    qb, kb, vb = (COL[k] // LANES for k in ("a_q", "a_k", "a_v"))
    return pl.pallas_call(
        functools.partial(_sb_prompt_kernel, tq=tq),
        grid=(b, W_GRP // LANES, t // tq),
        in_specs=[pl.BlockSpec((1, tq, LANES), lambda bi, p, i: (bi, i, qb + p)),
                  pl.BlockSpec((1, t, LANES), lambda bi, p, i: (bi, 0, kb + p)),
                  pl.BlockSpec((1, t, LANES), lambda bi, p, i: (bi, 0, vb + p))],
        out_specs=pl.BlockSpec((1, tq, LANES), lambda bi, p, i: (bi, i, p)),
        out_shape=jax.ShapeDtypeStruct((b, t, W_GRP), F32),
        compiler_params=_params(("parallel", "parallel", "arbitrary")),
        name="sb_prompt",
    )(proj3, proj3, proj3)


def _cmp_kernel(xk_ref, xv_ref, wk_ref, wv_ref, ok_ref, ov_ref):
    ok_ref[...] = _dot(xk_ref[...].astype(BF16), wk_ref[...])
    ov_ref[...] = _dot(xv_ref[...].astype(BF16), wv_ref[...])


def _compress(xk, xv, wk_exp, wv_exp, layer, row0, rows):
    kdim = xk.shape[1]
    tr = min(rows, 256)
    off = row0 // tr
    return pl.pallas_call(
        _cmp_kernel,
        grid=(rows // tr,),
        in_specs=[pl.BlockSpec((tr, kdim), lambda i: (off + i, 0)),
                  pl.BlockSpec((tr, kdim), lambda i: (off + i, 0)),
                  pl.BlockSpec((None, kdim, LANES), lambda i: (layer, 0, 0)),
                  pl.BlockSpec((None, kdim, LANES), lambda i: (layer, 0, 0))],
        out_specs=[pl.BlockSpec((tr, LANES), lambda i: (i, 0))] * 2,
        out_shape=[jax.ShapeDtypeStruct((rows, LANES), F32)] * 2,
        compiler_params=_params(("parallel",), 48),
        name="compress",
    )(xk, xv, wk_exp, wv_exp)


def _cmp_cache_kernel(xk_ref, xv_ref, wk_ref, wv_ref, ok_ref, ov_ref, *, rows):
    acc_k = jnp.zeros((rows, ok_ref.shape[1]), F32)
    acc_v = jnp.zeros((rows, ov_ref.shape[1]), F32)
    for d in range(HEAD_DIM):
        acc_k = acc_k + _dot(xk_ref[pl.ds(d, rows, stride=HEAD_DIM), :].astype(BF16), wk_ref[d])
        acc_v = acc_v + _dot(xv_ref[pl.ds(d, rows, stride=HEAD_DIM), :].astype(BF16), wv_ref[d])
    ok_ref[...] = acc_k
    ov_ref[...] = acc_v


def _compress_cache(xk, xv, wk_t, wv_t, layer, n_rows):
    page = xk.shape[1]
    n_out = wk_t.shape[3]
    tr = min(n_rows, LANES)
    off = layer * (n_rows // tr)
    x_spec = pl.BlockSpec((tr * HEAD_DIM, page), lambda i: (off + i, 0))
    w_spec = pl.BlockSpec((None, HEAD_DIM, page, n_out), lambda i: (layer, 0, 0, 0))
    return pl.pallas_call(
        functools.partial(_cmp_cache_kernel, rows=tr),
        grid=(n_rows // tr,),
        in_specs=[x_spec, x_spec, w_spec, w_spec],
        out_specs=[pl.BlockSpec((tr, n_out), lambda i: (i, 0))] * 2,
        out_shape=[jax.ShapeDtypeStruct((n_rows, n_out), F32)] * 2,
        compiler_params=_params(("parallel",), 48),
        name="compress_cache",
    )(xk, xv, wk_t, wv_t)


def _head_slope(h):
    return 2.0 ** -(h + 1)


def _half_variant(x, x_swapped, kvh, par, low):
    src = x if kvh == par else x_swapped
    keep = low if par == 0 else jnp.logical_not(low)
    return jnp.where(keep, src, 0.0).astype(BF16)


def _nsa_cmp_kernel(q_ref, kc_ref, vc_ref, oc_ref, sel_ref, *, tq, ncmp):
    qt = pl.program_id(1)
    nsel = ncmp // 2
    ntop = min(N_SEL, nsel)
    half = LANES // 2

    def permuted(ref):
        ev = ref[0, pl.ds(0, nsel, stride=2), :]
        od = ref[0, pl.ds(1, nsel, stride=2), :]
        if nsel == half:
            return jnp.concatenate([ev, od], axis=0)
        pad = jnp.zeros((half - nsel, LANES), F32)
        return jnp.concatenate([ev, pad, od, pad], axis=0)

    kc = permuted(kc_ref)
    vc = permuted(vc_ref)
    kc_sw = pltpu.roll(kc, half, 1)
    vc_sw = pltpu.roll(vc, half, 1)
    low = _iota((LANES, LANES), 1) < half

    n_i = _iota((tq, LANES), 1)
    odd = n_i >= half
    slot = jnp.where(odd, n_i - half, n_i)
    cmp_blk = jnp.where(odd, 2 * slot + 1, 2 * slot)
    qpos = qt * tq + _iota((tq, LANES), 0)
    dist = (qpos - (cmp_blk * L_CMP + (L_CMP - 1))).astype(F32)
    ok = (dist >= 0) & (slot < nsel)

    q = q_ref[0] * SCALE
    o_pairs = [jnp.zeros((tq, LANES), F32) for _ in range(4)]
    for kvh in range(2):
        imp = jnp.zeros((tq, LANES), F32)
        for g in range(4):
            h = kvh * 4 + g
            par, pair = h % 2, h // 2
            qb = q[:, pair * LANES:(pair + 1) * LANES].astype(BF16)
            z = _dot_nt(qb, _half_variant(kc, kc_sw, kvh, par, low))
            s = jnp.where(ok, z - _head_slope(h) * dist, NEG_BIG)
            e = jnp.exp(s - jnp.max(s, axis=1, keepdims=True))
            p = jnp.where(ok, e / jnp.sum(e, axis=1, keepdims=True), 0.0)
            imp = imp + p
            o_pairs[pair] = o_pairs[pair] + _dot(p.astype(BF16), _half_variant(vc, vc_sw, kvh, par, low))
        imp_sel = imp + pltpu.roll(imp, half, 1)
        blk = n_i
        cur = qpos >> 6
        forced = (blk == 0) | (blk == cur) | (blk == cur - 1)
        valid = (blk * L_SEL <= qpos) & (blk < nsel)
        score = jnp.where(valid, imp_sel + jnp.where(forced, FORCE_BONUS, 0.0), -jnp.inf)
        rank = jnp.zeros((tq, LANES), F32)
        for i in range(nsel):
            si = score[:, i:i + 1]
            beats = (si > score) | ((si == score) & (blk > i))
            rank = rank + jnp.where(beats, 1.0, 0.0)
        sel_ref[0, kvh] = jnp.where(valid & (rank < ntop), 1.0, 0.0)
    oc_ref[0] = jnp.concatenate(o_pairs, axis=1)


def _nsa_cmp_prompt(proj3, kcmp, vcmp):
    b, t, _ = proj3.shape
    ncmp = kcmp.shape[1]
    tq = min(t, 256)
    qb = COL["b_q"] // W_GRP
    return pl.pallas_call(
        functools.partial(_nsa_cmp_kernel, tq=tq, ncmp=ncmp),
        grid=(b, t // tq),
        in_specs=[pl.BlockSpec((1, tq, W_GRP), lambda bi, i: (bi, i, qb)),
                  pl.BlockSpec((1, ncmp, LANES), lambda bi, i: (bi, 0, 0)),
                  pl.BlockSpec((1, ncmp, LANES), lambda bi, i: (bi, 0, 0))],
        out_specs=[pl.BlockSpec((1, tq, W_GRP), lambda bi, i: (bi, i, 0)),
                   pl.BlockSpec((1, 2, tq, LANES), lambda bi, i: (bi, 0, i, 0))],
        out_shape=[jax.ShapeDtypeStruct((b, t, W_GRP), F32),
                   jax.ShapeDtypeStruct((b, 2, t, LANES), F32)],
        compiler_params=_params(("parallel", "parallel")),
        name="nsa_cmp_prompt",
    )(proj3, kcmp, vcmp)


def _nsa_att_kernel(q_ref, ks_ref, vs_ref, kw_ref, vw_ref, sel_ref, br_ref, oc_ref, o_ref,
                    ksv, vsv, kwv, vwv, *, tq, t):
    qt = pl.program_id(1)

    @pl.when(qt == 0)
    def _():
        low = _iota((t, LANES), 1) < LANES // 2
        for src, dst in ((ks_ref, ksv), (vs_ref, vsv), (kw_ref, kwv), (vw_ref, vwv)):
            x = src[0]
            xs = pltpu.roll(x, LANES // 2, 1)
            for kvh in range(2):
                for par in range(2):
                    dst[kvh * 2 + par] = _half_variant(x, xs, kvh, par, low)

    q = q_ref[0] * SCALE
    sig = _sigmoid(br_ref[0])
    oc = oc_ref[0]
    rel = (_iota((tq, tq), 0) - _iota((tq, tq), 1)).astype(F32)
    lane = _iota((tq, LANES), 1)
    blk_row = _iota((LANES, tq), 0)
    blk_col = _iota((LANES, tq), 1)
    init1 = (jnp.full((tq, 1), NEG_BIG, F32), jnp.zeros((tq, 1), F32), jnp.zeros((tq, LANES), F32))
    n_win = jnp.minimum(qt, (WINDOW + tq - 1) // tq) + 1

    for pair in range(4):
        kvh = pair // 2
        qb = q[:, pair * LANES:(pair + 1) * LANES].astype(BF16)
        oc_pair = oc[:, pair * LANES:(pair + 1) * LANES]
        selk = sel_ref[0, kvh].astype(BF16)

        def tile(kt, state, kv, vv, selected, pair=pair, kvh=kvh, qb=qb, selk=selk):
            start = pl.multiple_of(kt * tq, tq)
            dist = rel + ((qt - kt) * tq).astype(F32)
            if selected:
                expand = jnp.where(blk_row == ((start + blk_col) >> 6), 1.0, 0.0).astype(BF16)
                mask = (_dot(selk, expand) > 0.5) & (dist >= 0)
            else:
                mask = (dist >= 0) & (dist < WINDOW)
            new = []
            for par, (m, l, acc) in enumerate(state):
                vi = kvh * 2 + par
                k = kv[vi, pl.ds(start, tq), :]
                v = vv[vi, pl.ds(start, tq), :]
                s = _dot_nt(qb, k) - _head_slope(pair * 2 + par) * dist
                m_new = jnp.maximum(m, jnp.max(jnp.where(mask, s, NEG_BIG), axis=1, keepdims=True))
                p = jnp.where(mask, jnp.exp(s - m_new), 0.0)
                alpha = jnp.exp(m - m_new)
                new.append((m_new, alpha * l + jnp.sum(p, axis=1, keepdims=True),
                            alpha * acc + _dot(p.astype(BF16), v)))
            return tuple(new)

        st_s = lax.fori_loop(0, qt + 1, lambda i, st: tile(qt - i, st, ksv, vsv, True), (init1, init1))
        st_w = lax.fori_loop(0, n_win, lambda i, st: tile(qt - i, st, kwv, vwv, False), (init1, init1))
        o_heads = []
        for par in range(2):
            h = pair * 2 + par
            g_c = sig[:, 3 * h:3 * h + 1]
            g_s = sig[:, 3 * h + 1:3 * h + 2]
            g_w = sig[:, 3 * h + 2:3 * h + 3]
            o_heads.append(g_c * oc_pair + g_s * (st_s[par][2] / st_s[par][1])
                           + g_w * (st_w[par][2] / st_w[par][1]))
        o_ref[0, :, pair * LANES:(pair + 1) * LANES] = jnp.where(lane < HEAD_DIM, o_heads[0], o_heads[1])


def _nsa_att_prompt(proj3, sel, oc):
    b, t, _ = proj3.shape
    tq = min(t, 256)
    qb = COL["b_q"] // W_GRP
    kv_blocks = [COL[k] // LANES for k in ("b_ks", "b_vs", "b_kw", "b_vw")]
    br = COL["b_br"] // LANES
    in_specs = [pl.BlockSpec((1, tq, W_GRP), lambda bi, i: (bi, i, qb))]
    in_specs += [pl.BlockSpec((1, t, LANES), functools.partial(lambda bi, i, cb: (bi, 0, cb), cb=cb))
                 for cb in kv_blocks]
    in_specs += [pl.BlockSpec((1, 2, tq, LANES), lambda bi, i: (bi, 0, i, 0)),
                 pl.BlockSpec((1, tq, LANES), lambda bi, i: (bi, i, br)),
                 pl.BlockSpec((1, tq, W_GRP), lambda bi, i: (bi, i, 0))]
    return pl.pallas_call(
        functools.partial(_nsa_att_kernel, tq=tq, t=t),
        grid=(b, t // tq),
        in_specs=in_specs,
        out_specs=pl.BlockSpec((1, tq, W_GRP), lambda bi, i: (bi, i, 0)),
        out_shape=jax.ShapeDtypeStruct((b, t, W_GRP), F32),
        scratch_shapes=[pltpu.VMEM((4, t, LANES), BF16)] * 4,
        compiler_params=_params(("parallel", "arbitrary"), 48),
        name="nsa_att_prompt",
    )(proj3, proj3, proj3, proj3, proj3, sel, proj3, oc)


def _pool_kernel(*refs, tt, start, has_halo):
    if has_halo:
        u_ref, halo_ref, pre_ref, w_ref, sc_ref, o_ref, e_ref, d_ref = refs
    else:
        u_ref, pre_ref, w_ref, sc_ref, o_ref, e_ref, d_ref = refs
    ti = pl.program_id(1)
    e_ref[POOL_HALO:POOL_HALO + tt, :] = u_ref[0]
    if has_halo:
        @pl.when(ti == 0)
        def _():
            e_ref[0:POOL_HALO, :] = pre_ref[0]

        @pl.when(ti > 0)
        def _():
            e_ref[0:POOL_HALO, :] = halo_ref[0]
    else:
        e_ref[0:POOL_HALO, :] = pre_ref[0]

    ch = min(tt, 64)
    for c in range(tt // ch):
        r0 = c * ch
        pos = start + ti * tt + r0 + _iota((ch, LANES), 0)
        for g, win in enumerate(POOL_WINDOWS):
            lanes = slice(g * LANES, (g + 1) * LANES)
            cur = e_ref[POOL_HALO + r0:POOL_HALO + r0 + ch, lanes]
            total = cur
            for s in range(1, win):
                total = total + e_ref[POOL_HALO + r0 - s:POOL_HALO + r0 - s + ch, lanes]
            cnt = jnp.minimum(win, pos + 1).astype(F32)
            d_ref[r0:r0 + ch, lanes] = total / cnt - cur
    for g in range(len(POOL_WINDOWS)):
        lanes = slice(g * LANES, (g + 1) * LANES)
        y = _dot(d_ref[:, lanes].astype(BF16), w_ref[g])
        o_ref[0, :, lanes] = y * sc_ref[:, lanes]


def _pool(u_arr, cb, prefix, w_pool, pool_scale, layer, start):
    b, t, _ = u_arr.shape
    tt = min(t, 512)
    nt = t // tt
    has_halo = nt > 1
    in_specs = [pl.BlockSpec((1, tt, W_GRP), lambda bi, i: (bi, i, cb))]
    args = [u_arr]
    if has_halo:
        per = tt // POOL_HALO
        in_specs.append(pl.BlockSpec((1, POOL_HALO, W_GRP),
                                     lambda bi, i: (bi, jnp.maximum(i * per - 1, 0), cb)))
        args.append(u_arr)
    in_specs += [pl.BlockSpec((1, POOL_HALO, W_GRP), lambda bi, i: (bi, 0, 0)),
                 pl.BlockSpec((None, 4, LANES, LANES), lambda bi, i: (layer, 0, 0, 0)),
                 pl.BlockSpec((None, 1, W_GRP), lambda bi, i: (layer, 0, 0))]
    args += [prefix, w_pool, pool_scale]
    return pl.pallas_call(
        functools.partial(_pool_kernel, tt=tt, start=start, has_halo=has_halo),
        grid=(b, nt),
        in_specs=in_specs,
        out_specs=pl.BlockSpec((1, tt, W_GRP), lambda bi, i: (bi, i, 0)),
        out_shape=jax.ShapeDtypeStruct((b, t, W_GRP), F32),
        scratch_shapes=[pltpu.VMEM((POOL_HALO + tt, W_GRP), F32), pltpu.VMEM((tt, W_GRP), F32)],
        compiler_params=_params(("parallel", "arbitrary")),
        name="pool",
    )(*args)


def _conv_kernel(*refs, tt, has_halo):
    if has_halo:
        (a_ref, g_ref, ha_ref, hg_ref, pre_ref, cw_ref, cb_ref, lg_ref, lb_ref, pw_ref,
         o_ref, u_ref, e_ref, y_ref) = refs
    else:
        (a_ref, g_ref, pre_ref, cw_ref, cb_ref, lg_ref, lb_ref, pw_ref,
         o_ref, u_ref, e_ref, y_ref) = refs
    ti = pl.program_id(1)
    u = a_ref[0] * _sigmoid(g_ref[0])
    u_ref[0] = u
    e_ref[CONV_HALO:CONV_HALO + tt, :] = u
    if has_halo:
        @pl.when(ti == 0)
        def _():
            e_ref[0:CONV_HALO, :] = pre_ref[0]

        @pl.when(ti > 0)
        def _():
            e_ref[0:CONV_HALO, :] = ha_ref[0] * _sigmoid(hg_ref[0])
    else:
        e_ref[0:CONV_HALO, :] = pre_ref[0]

    ch = min(tt, 32)
    lead = CONV_HALO - (CONV_W - 1)
    for c in range(tt // ch):
        r0 = c * ch
        acc = jnp.broadcast_to(cb_ref[...], (ch, W_GRP))
        for j in range(CONV_W):
            acc = acc + cw_ref[j:j + 1, :] * e_ref[r0 + lead + j:r0 + lead + j + ch, :]
        y_ref[r0:r0 + ch, :] = acc
    y = y_ref[...]
    yc = y - jnp.mean(y, axis=-1, keepdims=True)
    yn = yc * lax.rsqrt(jnp.mean(yc * yc, axis=-1, keepdims=True) + LN_EPS)
    yn = yn * lg_ref[...] + lb_ref[...]
    o_ref[0] = _dot((yn * _sigmoid(yn)).astype(BF16), pw_ref[...])


def _conv(arr, cb_a, cb_g, prefix, conv_w, conv_b, ln_g, ln_b, w_pw, layer):
    b, t, _ = arr.shape
    tt = min(t, 512)
    nt = t // tt
    has_halo = nt > 1
    in_specs = [pl.BlockSpec((1, tt, W_GRP), lambda bi, i: (bi, i, cb_a)),
                pl.BlockSpec((1, tt, W_GRP), lambda bi, i: (bi, i, cb_g))]
    args = [arr, arr]
    if has_halo:
        per = tt // CONV_HALO
        in_specs += [pl.BlockSpec((1, CONV_HALO, W_GRP),
                                  lambda bi, i: (bi, jnp.maximum(i * per - 1, 0), cb_a)),
                     pl.BlockSpec((1, CONV_HALO, W_GRP),
                                  lambda bi, i: (bi, jnp.maximum(i * per - 1, 0), cb_g))]
        args += [arr, arr]
    vec = lambda: pl.BlockSpec((None, 1, W_GRP), lambda bi, i: (layer, 0, 0))
    in_specs += [pl.BlockSpec((1, CONV_HALO, W_GRP), lambda bi, i: (bi, 0, 0)),
                 pl.BlockSpec((None, CONV_HALO, W_GRP), lambda bi, i: (layer, 0, 0)),
                 vec(), vec(), vec(),
                 pl.BlockSpec((None, W_GRP, W_GRP), lambda bi, i: (layer, 0, 0))]
    args += [prefix, conv_w, conv_b, ln_g, ln_b, w_pw]
    return pl.pallas_call(
        functools.partial(_conv_kernel, tt=tt, has_halo=has_halo),
        grid=(b, nt),
        in_specs=in_specs,
        out_specs=[pl.BlockSpec((1, tt, W_GRP), lambda bi, i: (bi, i, 0))] * 2,
        out_shape=[jax.ShapeDtypeStruct((b, t, W_GRP), F32)] * 2,
        scratch_shapes=[pltpu.VMEM((CONV_HALO + tt, W_GRP), F32), pltpu.VMEM((tt, W_GRP), F32)],
        compiler_params=_params(("parallel", "arbitrary")),
        name="conv",
    )(*args)


SB_ROWS = 32
PAGES_PER_STEP = 8


def _sb_sample_kernel(pt_ref, q_ref, kn_ref, vn_ref, *rest, npg, page):
    del pt_ref
    krefs, vrefs = rest[:npg], rest[npg:2 * npg]
    o_ref, carry_ref, acc_ref = rest[2 * npg:]
    step = pl.program_id(1)
    qbd = q_ref[0].astype(BF16)
    upper = _upper_ones(page)

    def absorb(k, v, mask, channel_major):
        kb, vb = k.astype(BF16), v.astype(BF16)
        z = _dot(qbd, kb) if channel_major else _dot_nt(qbd, kb)
        lk = _log_keep(z)
        if mask is not None:
            lk = jnp.where(mask, lk, 0.0)
        between = _split_dot(lk, upper) + carry_ref[...]
        a = jnp.exp(z + lk + between)
        if mask is not None:
            a = jnp.where(mask, a, 0.0)
        ab = a.astype(BF16)
        acc_ref[...] += _dot_nt(ab, vb) if channel_major else _dot(ab, vb)
        carry_ref[...] += jnp.sum(lk, axis=1, keepdims=True)

    @pl.when(step == 0)
    def _():
        carry_ref[...] = jnp.zeros_like(carry_ref)
        acc_ref[...] = jnp.zeros_like(acc_ref)
        pad = jnp.zeros((page - kn_ref.shape[1], W_GRP), F32)
        r = _iota((SB_ROWS, page), 0)
        j = _iota((SB_ROWS, page), 1)
        absorb(jnp.concatenate([kn_ref[0], pad], axis=0),
               jnp.concatenate([vn_ref[0], pad], axis=0), j < (r >> 3), False)

    for g in range(npg):
        absorb(krefs[g][0, 0].reshape(W_GRP, page), vrefs[g][0, 0].reshape(W_GRP, page), None, True)

    @pl.when(step == pl.num_programs(1) - 1)
    def _():
        acc = acc_ref[...]
        r = _iota((SB_ROWS, HEAD_DIM), 0)
        out = jnp.zeros((SB_ROWS, HEAD_DIM), F32)
        for h in range(W_GRP // HEAD_DIM):
            out = jnp.where((r & 7) == h, acc[:, h * HEAD_DIM:(h + 1) * HEAD_DIM], out)
        o_ref[0] = out


def _sb_sample(page_table, qbd, k_new, v_new, cache_k, cache_v, layer):
    db, n_pages = page_table.shape
    n_heads, page = cache_k.shape[2], cache_k.shape[4]
    npg = min(PAGES_PER_STEP, n_pages)
    steps = n_pages // npg

    def page_spec(g):
        def index(bi, s, pt):
            return (layer, pt[bi * n_pages + (n_pages - 1 - (s * npg + g))], 0, 0, 0)
        return pl.BlockSpec((1, 1, n_heads, HEAD_DIM, page), index)

    small = lambda r: pl.BlockSpec((1, r, W_GRP), lambda bi, s, pt: (bi, 0, 0))
    grid_spec = pltpu.PrefetchScalarGridSpec(
        num_scalar_prefetch=1,
        grid=(db, steps),
        in_specs=[small(SB_ROWS), small(k_new.shape[1]), small(v_new.shape[1])]
        + [page_spec(g) for g in range(npg)] * 2,
        out_specs=pl.BlockSpec((1, SB_ROWS, HEAD_DIM), lambda bi, s, pt: (bi, 0, 0)),
        scratch_shapes=[pltpu.VMEM((SB_ROWS, 1), F32), pltpu.VMEM((SB_ROWS, W_GRP), F32)],
    )
    return pl.pallas_call(
        functools.partial(_sb_sample_kernel, npg=npg, page=page),
        grid_spec=grid_spec,
        out_shape=jax.ShapeDtypeStruct((db, SB_ROWS, HEAD_DIM), F32),
        compiler_params=_params(("parallel", "arbitrary")),
        name="sb_sample",
    )(page_table.reshape(-1), qbd, k_new, v_new, *([cache_k] * npg), *([cache_v] * npg))


def _row_fields(shape):
    r = _iota(shape, 0)
    qi = (r >> 1) & 3
    kvh = r & 1
    h = kvh * 4 + (r >> 3)
    slope = jnp.zeros(shape, F32)
    for hh in range(8):
        slope = jnp.where(h == hh, _head_slope(hh), slope)
    return qi, kvh, slope


def _pick_half(x, kvh64):
    return jnp.where(kvh64 == 0, x[:, :HEAD_DIM], x[:, HEAD_DIM:])


def _nsa_small_kernel(pt_ref, q_ref, kall_ref, vall_ref, kwc_ref, vwc_ref, kwn_ref, vwn_ref,
                      oc_ref, ow_ref, sel_ref, kg_ref, vg_ref, *, n_pages, past):
    b = pl.program_id(0)

    def gather(p, carry):
        idx = pt_ref[b * n_pages + p]
        kg_ref[pl.ds(p, 1), :] = kall_ref[pl.ds(idx, 1), :]
        vg_ref[pl.ds(p, 1), :] = vall_ref[pl.ds(idx, 1), :]
        return carry

    lax.fori_loop(0, n_pages, gather, 0)

    q = q_ref[0].astype(BF16)
    rows = q_ref.shape[1]
    _, kvh64, _ = _row_fields((rows, HEAD_DIM))

    qi, _, slope = _row_fields((rows, n_pages))
    pidx = _iota((rows, n_pages), 1)
    qpos = past + qi
    per_page = kg_ref.shape[1] // LANES
    scores, oks = [], []
    for n4 in range(per_page):
        z = _dot_nt(q, kg_ref[:, n4 * LANES:(n4 + 1) * LANES].astype(BF16))
        dist = (qpos - ((pidx * per_page + n4) * L_CMP + (L_CMP - 1))).astype(F32)
        ok = dist >= 0
        oks.append(ok)
        scores.append(jnp.where(ok, z - slope * dist, NEG_BIG))
    m = functools.reduce(jnp.maximum, [jnp.max(s, axis=1, keepdims=True) for s in scores])
    es = [jnp.exp(s - m) for s in scores]
    tot = functools.reduce(jnp.add, [jnp.sum(e, axis=1, keepdims=True) for e in es])
    ps = [jnp.where(ok, e / tot, 0.0) for ok, e in zip(oks, es)]
    o_c = functools.reduce(jnp.add, [
        _dot(p.astype(BF16), vg_ref[:, n4 * LANES:(n4 + 1) * LANES].astype(BF16))
        for n4, p in enumerate(ps)])
    oc_ref[0] = _pick_half(o_c, kvh64)

    def group_sum(x):
        return x[0:8] + x[8:16] + x[16:24] + x[24:32]

    lane = _iota((8, n_pages), 1)
    s_even = group_sum(ps[0] + ps[1]) + jnp.where(lane == 0, FORCE_BONUS, 0.0)
    s_odd = group_sum(ps[2] + ps[3]) + jnp.where(lane == n_pages - 1, FORCE_BONUS, 0.0)
    score = jnp.concatenate([s_even, s_odd], axis=1)
    lane2 = _iota((8, 2 * n_pages), 1)
    blk = jnp.where(lane2 < n_pages, 2 * lane2, 2 * (lane2 - n_pages) + 1)
    sel = jnp.zeros((8, 2 * n_pages), F32)
    for _ in range(N_SEL - 1):
        best = jnp.max(score, axis=1, keepdims=True)
        first = jnp.min(jnp.where(score == best, blk, 2 ** 30), axis=1, keepdims=True)
        hit = (blk == first) & (best > -jnp.inf)
        sel = jnp.where(hit, 1.0, sel)
        score = jnp.where(blk == first, -jnp.inf, score)
    sel_ref[0] = sel

    wbuf = kwc_ref.shape[4]
    n_new = kwn_ref.shape[1]
    padw = jnp.zeros((LANES - n_new, LANES), F32)
    kw_new = jnp.concatenate([kwn_ref[0], padw], axis=0).astype(BF16)
    vw_new = jnp.concatenate([vwn_ref[0], padw], axis=0).astype(BF16)
    kw_old = kwc_ref[0, 0].reshape(LANES, wbuf).astype(BF16)
    vw_old = vwc_ref[0, 0].reshape(LANES, wbuf).astype(BF16)

    def scores(z, dist):
        ok = (dist >= 0) & (dist < WINDOW)
        return jnp.where(ok, z - slope_of(dist.shape) * dist.astype(F32), NEG_BIG), ok

    def slope_of(shape):
        return _row_fields(shape)[2]

    qi_o, _, _ = _row_fields((rows, wbuf))
    qi_n, _, _ = _row_fields((rows, LANES))
    s_o, ok_o = scores(_dot(q, kw_old), wbuf + qi_o - _iota((rows, wbuf), 1))
    s_n, ok_n = scores(_dot_nt(q, kw_new), qi_n - _iota((rows, LANES), 1))
    m = jnp.maximum(jnp.max(s_o, axis=1, keepdims=True), jnp.max(s_n, axis=1, keepdims=True))
    e_o, e_n = jnp.exp(s_o - m), jnp.exp(s_n - m)
    tot = jnp.sum(e_o, axis=1, keepdims=True) + jnp.sum(e_n, axis=1, keepdims=True)
    p_o = jnp.where(ok_o, e_o / tot, 0.0).astype(BF16)
    p_n = jnp.where(ok_n, e_n / tot, 0.0).astype(BF16)
    ow_ref[0] = _pick_half(_dot_nt(p_o, vw_old) + _dot(p_n, vw_new), kvh64)


def _nsa_small(page_table, q_rows, kall, vall, win_k, win_v, kw_new, vw_new, layer, past):
    db, n_pages = page_table.shape
    rows = q_rows.shape[1]
    wbuf = win_k.shape[4]
    whole = lambda a: pl.BlockSpec(a.shape, lambda bi, pt: (0,) * a.ndim)
    per_b = lambda a: pl.BlockSpec((1,) + a.shape[1:], lambda bi, pt: (bi,) + (0,) * (a.ndim - 1))
    win = pl.BlockSpec((1, 1, 2, HEAD_DIM, wbuf), lambda bi, pt: (layer, bi, 0, 0, 0))
    grid_spec = pltpu.PrefetchScalarGridSpec(
        num_scalar_prefetch=1,
        grid=(db,),
        in_specs=[per_b(q_rows), whole(kall), whole(vall), win, win, per_b(kw_new), per_b(vw_new)],
        out_specs=[pl.BlockSpec((1, rows, HEAD_DIM), lambda bi, pt: (bi, 0, 0)),
                   pl.BlockSpec((1, rows, HEAD_DIM), lambda bi, pt: (bi, 0, 0)),
                   pl.BlockSpec((1, 8, 2 * n_pages), lambda bi, pt: (bi, 0, 0))],
        scratch_shapes=[pltpu.VMEM((n_pages, kall.shape[1]), F32)] * 2,
    )
    return pl.pallas_call(
        functools.partial(_nsa_small_kernel, n_pages=n_pages, past=past),
        grid_spec=grid_spec,
        out_shape=[jax.ShapeDtypeStruct((db, rows, HEAD_DIM), F32),
                   jax.ShapeDtypeStruct((db, rows, HEAD_DIM), F32),
                   jax.ShapeDtypeStruct((db, 8, 2 * n_pages), F32)],
        compiler_params=_params(("arbitrary",)),
        name="nsa_small_sample",
    )(page_table.reshape(-1), q_rows, kall, vall, win_k, win_v, kw_new, vw_new)


def _nsa_sel_kernel(pt_ref, q_ref, sel_ref, kn_ref, vn_ref, oc_ref, ow_ref, br_ref, *rest,
                    npg, n_pages, page, past):
    del pt_ref
    krefs, vrefs = rest[:npg], rest[npg:2 * npg]
    o_ref, m_ref, l_ref, acc_ref = rest[2 * npg:]
    step = pl.program_id(1)
    q = q_ref[0].astype(BF16)
    rows = q_ref.shape[1]
    qi, _, slope = _row_fields((rows, page))
    lane = _iota((rows, page), 1)
    qpos = past + qi

    def absorb(k, v, kpos0, mask, channel_major):
        kb, vb = k.astype(BF16), v.astype(BF16)
        dist = (qpos - (kpos0 + lane)).astype(F32)
        s = (_dot(q, kb) if channel_major else _dot_nt(q, kb)) - slope * dist
        if mask is None:
            mask = dist >= 0
        m_old = m_ref[...]
        m_new = jnp.maximum(m_old, jnp.max(jnp.where(mask, s, NEG_BIG), axis=1, keepdims=True))
        p = jnp.where(mask, jnp.exp(s - m_new), 0.0)
        alpha = jnp.exp(m_old - m_new)
        l_ref[...] = alpha * l_ref[...] + jnp.sum(p, axis=1, keepdims=True)
        pb = p.astype(BF16)
        acc_ref[...] = alpha * acc_ref[...] + (_dot_nt(pb, vb) if channel_major else _dot(pb, vb))
        m_ref[...] = m_new

    @pl.when(step == 0)
    def _():
        m_ref[...] = jnp.full_like(m_ref, NEG_BIG)
        l_ref[...] = jnp.zeros_like(l_ref)
        acc_ref[...] = jnp.zeros_like(acc_ref)
        pad = jnp.zeros((page - kn_ref.shape[1], LANES), F32)
        absorb(jnp.concatenate([kn_ref[0], pad], axis=0),
               jnp.concatenate([vn_ref[0], pad], axis=0), past, None, False)

    selm = jnp.concatenate([sel_ref[0]] * (rows // 8), axis=0)
    pl_lane = _iota((rows, n_pages), 1)
    per_blk = page // L_SEL
    for g in range(npg):
        pg = n_pages - 1 - (step * npg + g)
        mask = jnp.zeros((rows, page), jnp.bool_)
        for j in range(per_blk):
            cj = jnp.sum(jnp.where(pl_lane == pg, selm[:, j * n_pages:(j + 1) * n_pages], 0.0),
                         axis=1, keepdims=True)
            mask = mask | ((cj > 0.5) & (lane >= j * L_SEL) & (lane < (j + 1) * L_SEL))
        absorb(krefs[g][0, 0].reshape(LANES, page), vrefs[g][0, 0].reshape(LANES, page),
               pg * page, mask, True)

    @pl.when(step == pl.num_programs(1) - 1)
    def _():
        _, kvh64, _ = _row_fields((rows, HEAD_DIM))
        o_s = _pick_half(acc_ref[...] / l_ref[...], kvh64)
        sig = _sigmoid(br_ref[0])
        o_ref[0] = sig[:, 0:1] * oc_ref[0] + sig[:, 1:2] * o_s + sig[:, 2:3] * ow_ref[0]


def _nsa_sel_sample(page_table, q_rows, sel, k_new, v_new, oc, ow, br_rows, cache_k, cache_v,
                    layer, past):
    db, n_pages = page_table.shape
    rows = q_rows.shape[1]
    page = cache_k.shape[4]
    npg = min(PAGES_PER_STEP, n_pages)
    steps = n_pages // npg

    def page_spec(g):
        def index(bi, s, pt):
            return (layer, pt[bi * n_pages + (n_pages - 1 - (s * npg + g))], 0, 0, 0)
        return pl.BlockSpec((1, 1, 2, HEAD_DIM, page), index)

    per_b = lambda a: pl.BlockSpec((1,) + a.shape[1:], lambda bi, s, pt: (bi,) + (0,) * (a.ndim - 1))
    grid_spec = pltpu.PrefetchScalarGridSpec(
        num_scalar_prefetch=1,
        grid=(db, steps),
        in_specs=[per_b(a) for a in (q_rows, sel, k_new, v_new, oc, ow, br_rows)]
        + [page_spec(g) for g in range(npg)] * 2,
        out_specs=pl.BlockSpec((1, rows, HEAD_DIM), lambda bi, s, pt: (bi, 0, 0)),
        scratch_shapes=[pltpu.VMEM((rows, 1), F32), pltpu.VMEM((rows, 1), F32),
                        pltpu.VMEM((rows, LANES), F32)],
    )
    return pl.pallas_call(
        functools.partial(_nsa_sel_kernel, npg=npg, n_pages=n_pages, page=page, past=past),
        grid_spec=grid_spec,
        out_shape=jax.ShapeDtypeStruct((db, rows, HEAD_DIM), F32),
        compiler_params=_params(("parallel", "arbitrary")),
        name="nsa_sel_sample",
    )(page_table.reshape(-1), q_rows, sel, k_new, v_new, oc, ow, br_rows,
      *([cache_k] * npg), *([cache_v] * npg))


def _pack_w_in(w_in):
    depth, d, _ = w_in.shape
    pieces, src = [], 0
    placed = {}
    for name, width in _SRC:
        placed[name] = (src, width)
        src += width
    order = sorted(COL, key=COL.get)
    pos = 0
    for name in order:
        assert COL[name] == pos
        s, width = placed[name]
        pieces.append(w_in[:, :, s:s + width])
        pos += width
    pieces.append(jnp.zeros((depth, d, N_PACK - pos), w_in.dtype))
    return jnp.concatenate(pieces, axis=2).astype(BF16)


def _expand_cmp_weight(w):
    depth, l, d, e = w.shape
    eye = jnp.eye(2, dtype=w.dtype)
    return jnp.einsum("zlde,hg->zlhdge", w, eye).reshape(depth, l * 2 * d, 2 * e).astype(BF16)


def _expand_cmp_weight_t(w, page):
    depth, l, d, e = w.shape
    n = page // l
    eye = jnp.eye(n, dtype=w.dtype)
    return jnp.einsum("zlde,nm->zdnlme", w, eye).reshape(depth, d, page, n * e).astype(BF16)


def _sample_rows(x, n_heads_last):
    db, tq = x.shape[:2]
    x = x.reshape(db, tq, 2, 4, n_heads_last)
    return x.transpose(0, 3, 1, 2, 4).reshape(db, 4 * tq * 2, n_heads_last)


def kernel(x_prompt, x_sample, cache_sb_k, cache_sb_v, cache_cmp_k, cache_cmp_v, cache_sel_k, cache_sel_v, cache_win_k, cache_win_v, state_pool, state_conv, page_table, norm_g, w_in, w_cmp_k, w_cmp_v, w_pool, pool_scale, conv_w, conv_b, ln_g, ln_b, w_pw, w_out, final_g):
    bp, seq, d_model = x_prompt.shape
    db, tq_s, _ = x_sample.shape
    depth, n_pool, page = cache_sb_k.shape[:3]
    assert tq_s == 4 and page == LANES and cache_win_k.shape[2] == WINDOW and seq % 256 == 0

    w = _prepare_weights(norm_g, w_in, w_cmp_k, w_cmp_v, w_pool, pool_scale, conv_w, conv_b,
                         ln_g, ln_b, w_pw, w_out, final_g)
    chan_major = lambda c: jnp.transpose(c, (0, 1, 3, 4, 2))
    sbk, sbv = chan_major(cache_sb_k), chan_major(cache_sb_v)
    selk, selv = chan_major(cache_sel_k), chan_major(cache_sel_v)
    cmpk = chan_major(cache_cmp_k).reshape(depth * n_pool * 2 * HEAD_DIM, page)
    cmpv = chan_major(cache_cmp_v).reshape(depth * n_pool * 2 * HEAD_DIM, page)

    hp = x_prompt.reshape(bp * seq, d_model)
    hs = x_sample.reshape(db * tq_s, d_model)
    p_states, s_states = [], []
    for l in range(depth):
        last = l == depth - 1
        hp, st = _prompt_layer(hp, bp, seq, w, l, last)
        p_states.append(st)
        hs, st = _sample_layer(hs, db, tq_s, w, l, last, page_table, sbk, sbv, cmpk, cmpv, selk, selv,
                               cache_win_k, cache_win_v, state_pool, state_conv)
        s_states.append(st)

    stacked = lambda states, j: jnp.stack([st[j] for st in states], axis=0)
    return (hp.reshape(bp, seq, d_model), hs.reshape(db, tq_s, d_model),
            *[stacked(p_states, j) for j in range(10)],
            *[stacked(s_states, j) for j in range(10)])


def _prepare_weights(norm_g, w_in, w_cmp_k, w_cmp_v, w_pool, pool_scale, conv_w, conv_b, ln_g, ln_b,
                     w_pw, w_out, final_g):
    depth = w_in.shape[0]
    vec3 = lambda a: a.reshape(depth, 1, -1)
    return dict(
        w_pack=_pack_w_in(w_in), w_out=w_out.astype(BF16),
        wk_exp=_expand_cmp_weight(w_cmp_k), wv_exp=_expand_cmp_weight(w_cmp_v),
        wk_t=_expand_cmp_weight_t(w_cmp_k, LANES), wv_t=_expand_cmp_weight_t(w_cmp_v, LANES),
        w_pool=w_pool.astype(BF16), w_pw=w_pw.astype(BF16),
        conv_w=jnp.pad(conv_w, ((0, 0), (0, CONV_HALO - CONV_W), (0, 0))),
        norm_g=vec3(norm_g), pool_scale=vec3(pool_scale), conv_b=vec3(conv_b),
        ln_g=vec3(ln_g), ln_b=vec3(ln_b), final_g=final_g.reshape(1, -1))


def _prompt_layer(hp, bp, seq, w, l, last):
    proj = _inproj(hp, w["norm_g"], w["w_pack"], l)
    proj3 = proj.reshape(bp, seq, N_PACK)
    col = lambda name, width: proj3[:, :, COL[name]:COL[name] + width]
    kc, vc = col("b_kc", LANES), col("b_vc", LANES)
    o_a = _sb_prompt(proj3)
    n_blocks = bp * seq // L_CMP
    kcmp, vcmp = _compress(kc.reshape(n_blocks, L_CMP * LANES), vc.reshape(n_blocks, L_CMP * LANES),
                           w["wk_exp"], w["wv_exp"], l, 0, n_blocks)
    n_cmp = seq // L_CMP
    o_cmp, sel = _nsa_cmp_prompt(proj3, kcmp.reshape(bp, n_cmp, LANES), vcmp.reshape(bp, n_cmp, LANES))
    o_b = _nsa_att_prompt(proj3, sel, o_cmp)
    zero_pool = jnp.zeros((bp, POOL_HALO, W_GRP), F32)
    zero_conv = jnp.zeros((bp, CONV_HALO, W_GRP), F32)
    o_c = _pool(proj3, COL["c_u"] // W_GRP, zero_pool, w["w_pool"], w["pool_scale"], l, 0)
    o_d, u_conv = _conv(proj3, COL["d_a"] // W_GRP, COL["d_gl"] // W_GRP, zero_conv,
                        w["conv_w"], w["conv_b"], w["ln_g"], w["ln_b"], w["w_pw"], l)
    flat = lambda a: a.reshape(bp * seq, W_GRP)
    hp_new = _outproj([flat(o_a), flat(o_b), flat(o_c), flat(o_d)], proj, hp, w["w_out"], l,
                      w["final_g"] if last else None)
    keep = min(WINDOW, seq)
    heads = lambda a, n: a.reshape(bp, -1, n, HEAD_DIM)
    states = (
        heads(col("a_k", W_GRP), 8), heads(col("a_v", W_GRP), 8),
        heads(kc, 2), heads(vc, 2),
        heads(col("b_ks", LANES), 2), heads(col("b_vs", LANES), 2),
        heads(col("b_kw", LANES)[:, seq - keep:], 2), heads(col("b_vw", LANES)[:, seq - keep:], 2),
        col("c_u", W_GRP)[:, seq - (POOL_HALO - 1):], u_conv[:, seq - (CONV_W - 1):])
    return hp_new, states


def _sample_layer(hs, db, tq_s, w, l, last, page_table, sbk, sbv, cmpk, cmpv, selk, selv,
                  cache_win_k, cache_win_v, state_pool, state_conv):
    depth, n_pool = sbk.shape[:2]
    page = sbk.shape[4]
    n_pages = page_table.shape[1]
    past = n_pages * page
    projs = _inproj(hs, w["norm_g"], w["w_pack"], l)
    projs3 = projs.reshape(db, tq_s, N_PACK)
    scol = lambda name, width: projs3[:, :, COL[name]:COL[name] + width]
    pad8 = lambda a: jnp.pad(a, ((0, 0), (0, 8 - tq_s), (0, 0)))
    head_of_col = jnp.arange(W_GRP) // HEAD_DIM
    sb_mask = (head_of_col[None, :] == jnp.arange(8)[:, None]).astype(F32)
    a_q = scol("a_q", W_GRP) * SCALE
    qbd = (a_q[:, :, None, :] * sb_mask[None, None]).reshape(db, tq_s * 8, W_GRP)
    o_a = _sb_sample(page_table, qbd, pad8(scol("a_k", W_GRP)), pad8(scol("a_v", W_GRP)), sbk, sbv, l)
    o_a = o_a.reshape(db * tq_s, W_GRP)
    kall, vall = _compress_cache(cmpk, cmpv, w["wk_t"], w["wv_t"], l, n_pool * 2)
    per_page = page // L_CMP
    relane = lambda a: a.reshape(n_pool, 2, per_page, HEAD_DIM).transpose(0, 2, 1, 3).reshape(
        n_pool, per_page * LANES)
    kall, vall = relane(kall), relane(vall)
    b_q = _sample_rows(scol("b_q", W_GRP) * SCALE, HEAD_DIM)
    kvh_of_row = jnp.arange(b_q.shape[1]) & 1
    q_rows = (b_q[:, :, None, :] * jnp.eye(2, dtype=F32)[kvh_of_row][None, :, :, None]).reshape(db, -1, LANES)
    br_rows = jnp.pad(_sample_rows(scol("b_br", 24), 3), ((0, 0), (0, 0), (0, LANES - 3)))
    kw_new, vw_new = scol("b_kw", LANES), scol("b_vw", LANES)
    wink = jnp.transpose(cache_win_k, (0, 1, 3, 4, 2))
    winv = jnp.transpose(cache_win_v, (0, 1, 3, 4, 2))
    o_cmp, o_win, sel = _nsa_small(page_table, q_rows, kall, vall, wink, winv,
                                   pad8(kw_new), pad8(vw_new), l, past)
    o_b_rows = _nsa_sel_sample(page_table, q_rows, sel, pad8(scol("b_ks", LANES)),
                               pad8(scol("b_vs", LANES)), o_cmp, o_win, br_rows, selk, selv, l, past)
    o_b = o_b_rows.reshape(db, 4, tq_s, 2, HEAD_DIM).transpose(0, 2, 3, 1, 4).reshape(db * tq_s, W_GRP)
    pool_pre = jnp.pad(state_pool[l], ((0, 0), (POOL_HALO - state_pool.shape[2], 0), (0, 0)))
    o_c = _pool(pad8(scol("c_u", W_GRP)), 0, pool_pre, w["w_pool"], w["pool_scale"], l, past)
    o_c = o_c[:, :tq_s].reshape(db * tq_s, W_GRP)
    conv_pre = jnp.pad(state_conv[l], ((0, 0), (CONV_HALO - state_conv.shape[2], 0), (0, 0)))
    o_d, u_conv = _conv(pad8(scol("d_a", 2 * W_GRP)), 0, 1, conv_pre,
                        w["conv_w"], w["conv_b"], w["ln_g"], w["ln_b"], w["w_pw"], l)
    o_d = o_d[:, :tq_s].reshape(db * tq_s, W_GRP)
    hs_new = _outproj([o_a, o_b, o_c, o_d], projs, hs, w["w_out"], l, w["final_g"] if last else None)
    sheads = lambda a, n: a.reshape(db, -1, n, HEAD_DIM)
    states = (
        sheads(scol("a_k", W_GRP), 8), sheads(scol("a_v", W_GRP), 8),
        sheads(scol("b_kc", LANES), 2), sheads(scol("b_vc", LANES), 2),
        sheads(scol("b_ks", LANES), 2), sheads(scol("b_vs", LANES), 2),
        jnp.concatenate([cache_win_k[l][:, tq_s:], sheads(kw_new, 2)], axis=1),
        jnp.concatenate([cache_win_v[l][:, tq_s:], sheads(vw_new, 2)], axis=1),
        jnp.concatenate([state_pool[l][:, tq_s:], scol("c_u", W_GRP)], axis=1),
        jnp.concatenate([state_conv[l][:, tq_s:], u_conv[:, :tq_s]], axis=1))
    return hs_new, states
```

```python
import functools

import jax
import jax.numpy as jnp
from jax import lax
from jax.experimental import pallas as pl
from jax.experimental.pallas import tpu as pltpu

F32 = jnp.float32
BF16 = jnp.bfloat16

HEAD_DIM = 64
LANES = 128
W_GRP = 512
L_CMP = 32
L_SEL = 64
N_SEL = 16
WINDOW = 512
FORCE_BONUS = 1000.0
POOL_WINDOWS = (2, 4, 8, 16)
POOL_HALO = 16
CONV_W = 31
CONV_HALO = 32
RMS_EPS = 1e-6
LN_EPS = 1e-5
NEG_BIG = -1e30
SCALE = HEAD_DIM ** -0.5
SB_DEAD_CARRY = -104.0

COL = dict(a_q=0, a_k=512, a_v=1024, a_g=1536, b_q=2048, b_g=2560, c_u=3072, c_g=3584,
           d_a=4096, d_gl=4608, d_g=5120, b_kc=5632, b_vc=5760, b_ks=5888, b_vs=6016,
           b_kw=6144, b_vw=6272, b_br=6400)
N_PACK = 6656
_SRC = (("a_q", 512), ("a_k", 512), ("a_v", 512), ("a_g", 512), ("b_q", 512), ("b_kc", 128),
        ("b_vc", 128), ("b_ks", 128), ("b_vs", 128), ("b_kw", 128), ("b_vw", 128), ("b_g", 512),
        ("b_br", 24), ("c_u", 512), ("c_g", 512), ("d_a", 512), ("d_gl", 512), ("d_g", 512))


def _params(sem, vmem_mb=None):
    kw = dict(dimension_semantics=sem)
    if vmem_mb is not None:
        kw["vmem_limit_bytes"] = vmem_mb * 2 ** 20
    return pltpu.CompilerParams(**kw)


def _sigmoid(x):
    return 1.0 / (1.0 + jnp.exp(-x))


def _iota(shape, axis):
    return lax.broadcasted_iota(jnp.int32, shape, axis)


def _dot_nt(a, b):
    return lax.dot_general(a, b, (((1,), (1,)), ((), ())), preferred_element_type=F32)


def _dot(a, b):
    return jnp.dot(a, b, preferred_element_type=F32)


def _split_dot(x, m):
    hi = x.astype(BF16)
    lo = (x - hi.astype(F32)).astype(BF16)
    return _dot(hi, m) + _dot(lo, m)


def _log_keep(z):
    return jnp.minimum(-z, 0.0) - jnp.log(1.0 + jnp.exp(-jnp.abs(z)))


def _upper_ones(n):
    return jnp.where(_iota((n, n), 0) > _iota((n, n), 1), 1.0, 0.0).astype(BF16)


def _inproj_kernel(x_ref, g_ref, w_ref, o_ref, hn_ref):
    @pl.when(pl.program_id(1) == 0)
    def _():
        x = x_ref[...]
        ms = jnp.mean(x * x, axis=-1, keepdims=True)
        hn_ref[...] = ((x * lax.rsqrt(ms + RMS_EPS)) * g_ref[...]).astype(BF16)

    o_ref[...] = _dot(hn_ref[...], w_ref[...])


def _inproj(x, norm_g, w_pack, layer):
    rows, d = x.shape
    n = w_pack.shape[2]
    tm = min(rows, 1024)
    tn = n // 4 if (n // 4) % LANES == 0 else 512
    return pl.pallas_call(
        _inproj_kernel,
        grid=(rows // tm, n // tn),
        in_specs=[pl.BlockSpec((tm, d), lambda i, j: (i, 0)),
                  pl.BlockSpec((None, 1, d), lambda i, j: (layer, 0, 0)),
                  pl.BlockSpec((None, d, tn), lambda i, j: (layer, 0, j))],
        out_specs=pl.BlockSpec((tm, tn), lambda i, j: (i, j)),
        out_shape=jax.ShapeDtypeStruct((rows, n), F32),
        scratch_shapes=[pltpu.VMEM((tm, d), BF16)],
        compiler_params=_params(("parallel", "arbitrary"), 56),
        name="inproj",
    )(x, norm_g, w_pack)


def _outproj_kernel(oa, ob, oc, od, ga, gb, gc, gd, h_ref, w_ref, *rest, final):
    if final:
        fg_ref, o_ref = rest
    else:
        (o_ref,) = rest
    acc = h_ref[...]
    for i, (o, g) in enumerate(((oa, ga), (ob, gb), (oc, gc), (od, gd))):
        gv = g[...]
        mix = (o[...] * (gv * _sigmoid(gv))).astype(BF16)
        acc = acc + _dot(mix, w_ref[i * W_GRP:(i + 1) * W_GRP, :])
    if final:
        ms = jnp.mean(acc * acc, axis=-1, keepdims=True)
        acc = (acc * lax.rsqrt(ms + RMS_EPS)) * fg_ref[...]
    o_ref[...] = acc


def _outproj(outs, proj, h, w_out, layer, final_g=None):
    rows, d = h.shape
    tm = min(rows, 256)
    final = final_g is not None
    gate_blocks = [COL[k] // W_GRP for k in ("a_g", "b_g", "c_g", "d_g")]
    in_specs = [pl.BlockSpec((tm, W_GRP), lambda i: (i, 0)) for _ in range(4)]
    in_specs += [pl.BlockSpec((tm, W_GRP), functools.partial(lambda i, cb: (i, cb), cb=cb))
                 for cb in gate_blocks]
    in_specs += [pl.BlockSpec((tm, d), lambda i: (i, 0)),
                 pl.BlockSpec((None, 4 * W_GRP, d), lambda i: (layer, 0, 0))]
    args = list(outs) + [proj] * 4 + [h, w_out]
    if final:
        in_specs.append(pl.BlockSpec((1, d), lambda i: (0, 0)))
        args.append(final_g)
    return pl.pallas_call(
        functools.partial(_outproj_kernel, final=final),
        grid=(rows // tm,),
        in_specs=in_specs,
        out_specs=pl.BlockSpec((tm, d), lambda i: (i, 0)),
        out_shape=jax.ShapeDtypeStruct((rows, d), F32),
        compiler_params=_params(("parallel",), 48),
        name="outproj",
    )(*args)


def _sb_prompt_kernel(q_ref, k_ref, v_ref, o_ref, *, tq):
    qt = pl.program_id(2)
    lane = _iota((tq, LANES), 1)
    row = _iota((tq, tq), 0)
    col = _iota((tq, tq), 1)
    before = col < row
    upper = _upper_ones(tq)
    q = q_ref[0] * SCALE
    qms = [jnp.where((lane >= hh * HEAD_DIM) & (lane < (hh + 1) * HEAD_DIM), q, 0.0).astype(BF16)
           for hh in range(2)]

    def tile(kt, state, diag):
        start = pl.multiple_of(kt * tq, tq)
        k = k_ref[0, pl.ds(start, tq), :].astype(BF16)
        v = v_ref[0, pl.ds(start, tq), :].astype(BF16)
        new = []
        for qm, (carry, acc) in zip(qms, state):
            z = _dot_nt(qm, k)
            lk = _log_keep(z)
            if diag:
                lk = jnp.where(before, lk, 0.0)
            between = _split_dot(lk, upper) + carry
            a = jnp.exp(z + lk + between)
            if diag:
                a = jnp.where(before, a, 0.0)
            new.append((carry + jnp.sum(lk, axis=1, keepdims=True), acc + _dot(a.astype(BF16), v)))
        return tuple(new)

    def any_live(state):
        return jnp.maximum(jnp.max(state[0][0]), jnp.max(state[1][0])) > SB_DEAD_CARRY

    zero = (jnp.zeros((tq, 1), F32), jnp.zeros((tq, LANES), F32))
    state = tile(qt, (zero, zero), True)

    def cond(c):
        return (c[0] < qt) & c[1]

    def body(c):
        st = tile(qt - 1 - c[0], c[2], False)
        return c[0] + 1, any_live(st), st

    _, _, state = lax.while_loop(cond, body, (jnp.int32(0), any_live(state), state))
    o_ref[0] = jnp.where(lane < HEAD_DIM, state[0][1], state[1][1])


def _sb_prompt(proj3):
    b, t, _ = proj3.shape
    tq = min(t, 256)
    qb, kb, vb = (COL[k] // LANES for k in ("a_q", "a_k", "a_v"))
    return pl.pallas_call(
        functools.partial(_sb_prompt_kernel, tq=tq),
        grid=(b, W_GRP // LANES, t // tq),
        in_specs=[pl.BlockSpec((1, tq, LANES), lambda bi, p, i: (bi, i, qb + p)),
                  pl.BlockSpec((1, t, LANES), lambda bi, p, i: (bi, 0, kb + p)),
                  pl.BlockSpec((1, t, LANES), lambda bi, p, i: (bi, 0, vb + p))],
        out_specs=pl.BlockSpec((1, tq, LANES), lambda bi, p, i: (bi, i, p)),
        out_shape=jax.ShapeDtypeStruct((b, t, W_GRP), F32),
        compiler_params=_params(("parallel", "parallel", "arbitrary")),
        name="sb_prompt",
    )(proj3, proj3, proj3)


def _cmp_kernel(xk_ref, xv_ref, wk_ref, wv_ref, ok_ref, ov_ref):
    ok_ref[...] = _dot(xk_ref[...].astype(BF16), wk_ref[...])
    ov_ref[...] = _dot(xv_ref[...].astype(BF16), wv_ref[...])


def _compress(xk, xv, wk_exp, wv_exp, layer, row0, rows):
    kdim = xk.shape[1]
    tr = min(rows, 256)
    off = row0 // tr
    return pl.pallas_call(
        _cmp_kernel,
        grid=(rows // tr,),
        in_specs=[pl.BlockSpec((tr, kdim), lambda i: (off + i, 0)),
                  pl.BlockSpec((tr, kdim), lambda i: (off + i, 0)),
                  pl.BlockSpec((None, kdim, LANES), lambda i: (layer, 0, 0)),
                  pl.BlockSpec((None, kdim, LANES), lambda i: (layer, 0, 0))],
        out_specs=[pl.BlockSpec((tr, LANES), lambda i: (i, 0))] * 2,
        out_shape=[jax.ShapeDtypeStruct((rows, LANES), F32)] * 2,
        compiler_params=_params(("parallel",), 48),
        name="compress",
    )(xk, xv, wk_exp, wv_exp)


def _cmp_cache_kernel(xk_ref, xv_ref, wk_ref, wv_ref, ok_ref, ov_ref, *, rows):
    acc_k = jnp.zeros((rows, ok_ref.shape[1]), F32)
    acc_v = jnp.zeros((rows, ov_ref.shape[1]), F32)
    for d in range(HEAD_DIM):
        acc_k = acc_k + _dot(xk_ref[pl.ds(d, rows, stride=HEAD_DIM), :].astype(BF16), wk_ref[d])
        acc_v = acc_v + _dot(xv_ref[pl.ds(d, rows, stride=HEAD_DIM), :].astype(BF16), wv_ref[d])
    ok_ref[...] = acc_k
    ov_ref[...] = acc_v


def _compress_cache(xk, xv, wk_t, wv_t, layer, n_rows):
    page = xk.shape[1]
    n_out = wk_t.shape[3]
    tr = min(n_rows, LANES)
    off = layer * (n_rows // tr)
    x_spec = pl.BlockSpec((tr * HEAD_DIM, page), lambda i: (off + i, 0))
    w_spec = pl.BlockSpec((None, HEAD_DIM, page, n_out), lambda i: (layer, 0, 0, 0))
    return pl.pallas_call(
        functools.partial(_cmp_cache_kernel, rows=tr),
        grid=(n_rows // tr,),
        in_specs=[x_spec, x_spec, w_spec, w_spec],
        out_specs=[pl.BlockSpec((tr, n_out), lambda i: (i, 0))] * 2,
        out_shape=[jax.ShapeDtypeStruct((n_rows, n_out), F32)] * 2,
        compiler_params=_params(("parallel",), 48),
        name="compress_cache",
    )(xk, xv, wk_t, wv_t)


def _head_slope(h):
    return 2.0 ** -(h + 1)


def _half_variant(x, x_swapped, kvh, par, low):
    src = x if kvh == par else x_swapped
    keep = low if par == 0 else jnp.logical_not(low)
    return jnp.where(keep, src, 0.0).astype(BF16)


def _nsa_cmp_kernel(q_ref, kc_ref, vc_ref, oc_ref, sel_ref, *, tq, ncmp):
    qt = pl.program_id(1)
    nsel = ncmp // 2
    ntop = min(N_SEL, nsel)
    half = LANES // 2

    def permuted(ref):
        ev = ref[0, pl.ds(0, nsel, stride=2), :]
        od = ref[0, pl.ds(1, nsel, stride=2), :]
        if nsel == half:
            return jnp.concatenate([ev, od], axis=0)
        pad = jnp.zeros((half - nsel, LANES), F32)
        return jnp.concatenate([ev, pad, od, pad], axis=0)

    kc = permuted(kc_ref)
    vc = permuted(vc_ref)
    kc_sw = pltpu.roll(kc, half, 1)
    vc_sw = pltpu.roll(vc, half, 1)
    low = _iota((LANES, LANES), 1) < half

    n_i = _iota((tq, LANES), 1)
    odd = n_i >= half
    slot = jnp.where(odd, n_i - half, n_i)
    cmp_blk = jnp.where(odd, 2 * slot + 1, 2 * slot)
    qpos = qt * tq + _iota((tq, LANES), 0)
    dist = (qpos - (cmp_blk * L_CMP + (L_CMP - 1))).astype(F32)
    ok = (dist >= 0) & (slot < nsel)

    q = q_ref[0] * SCALE
    o_pairs = [jnp.zeros((tq, LANES), F32) for _ in range(4)]
    for kvh in range(2):
        imp = jnp.zeros((tq, LANES), F32)
        for g in range(4):
            h = kvh * 4 + g
            par, pair = h % 2, h // 2
            qb = q[:, pair * LANES:(pair + 1) * LANES].astype(BF16)
            z = _dot_nt(qb, _half_variant(kc, kc_sw, kvh, par, low))
            s = jnp.where(ok, z - _head_slope(h) * dist, NEG_BIG)
            e = jnp.exp(s - jnp.max(s, axis=1, keepdims=True))
            p = jnp.where(ok, e / jnp.sum(e, axis=1, keepdims=True), 0.0)
            imp = imp + p
            o_pairs[pair] = o_pairs[pair] + _dot(p.astype(BF16), _half_variant(vc, vc_sw, kvh, par, low))
        imp_sel = imp + pltpu.roll(imp, half, 1)
        blk = n_i
        cur = qpos >> 6
        forced = (blk == 0) | (blk == cur) | (blk == cur - 1)
        valid = (blk * L_SEL <= qpos) & (blk < nsel)
        score = jnp.where(valid, imp_sel + jnp.where(forced, FORCE_BONUS, 0.0), -jnp.inf)
        rank = jnp.zeros((tq, LANES), F32)
        for i in range(nsel):
            si = score[:, i:i + 1]
            beats = (si > score) | ((si == score) & (blk > i))
            rank = rank + jnp.where(beats, 1.0, 0.0)
        sel_ref[0, kvh] = jnp.where(valid & (rank < ntop), 1.0, 0.0)
    oc_ref[0] = jnp.concatenate(o_pairs, axis=1)


def _nsa_cmp_prompt(proj3, kcmp, vcmp):
    b, t, _ = proj3.shape
    ncmp = kcmp.shape[1]
    tq = min(t, 256)
    qb = COL["b_q"] // W_GRP
    return pl.pallas_call(
        functools.partial(_nsa_cmp_kernel, tq=tq, ncmp=ncmp),
        grid=(b, t // tq),
        in_specs=[pl.BlockSpec((1, tq, W_GRP), lambda bi, i: (bi, i, qb)),
                  pl.BlockSpec((1, ncmp, LANES), lambda bi, i: (bi, 0, 0)),
                  pl.BlockSpec((1, ncmp, LANES), lambda bi, i: (bi, 0, 0))],
        out_specs=[pl.BlockSpec((1, tq, W_GRP), lambda bi, i: (bi, i, 0)),
                   pl.BlockSpec((1, 2, tq, LANES), lambda bi, i: (bi, 0, i, 0))],
        out_shape=[jax.ShapeDtypeStruct((b, t, W_GRP), F32),
                   jax.ShapeDtypeStruct((b, 2, t, LANES), F32)],
        compiler_params=_params(("parallel", "parallel")),
        name="nsa_cmp_prompt",
    )(proj3, kcmp, vcmp)


def _nsa_att_kernel(q_ref, ks_ref, vs_ref, kw_ref, vw_ref, sel_ref, br_ref, oc_ref, o_ref,
                    ksv, vsv, kwv, vwv, *, tq, t):
    qt = pl.program_id(1)

    @pl.when(qt == 0)
    def _():
        low = _iota((t, LANES), 1) < LANES // 2
        for src, dst in ((ks_ref, ksv), (vs_ref, vsv), (kw_ref, kwv), (vw_ref, vwv)):
            x = src[0]
            xs = pltpu.roll(x, LANES // 2, 1)
            for kvh in range(2):
                for par in range(2):
                    dst[kvh * 2 + par] = _half_variant(x, xs, kvh, par, low)

    q = q_ref[0] * SCALE
    sig = _sigmoid(br_ref[0])
    oc = oc_ref[0]
    rel = (_iota((tq, tq), 0) - _iota((tq, tq), 1)).astype(F32)
    lane = _iota((tq, LANES), 1)
    blk_row = _iota((LANES, tq), 0)
    blk_col = _iota((LANES, tq), 1)
    init1 = (jnp.full((tq, 1), NEG_BIG, F32), jnp.zeros((tq, 1), F32), jnp.zeros((tq, LANES), F32))
    n_win = jnp.minimum(qt, (WINDOW + tq - 1) // tq) + 1

    for pair in range(4):
        kvh = pair // 2
        qb = q[:, pair * LANES:(pair + 1) * LANES].astype(BF16)
        oc_pair = oc[:, pair * LANES:(pair + 1) * LANES]
        selk = sel_ref[0, kvh].astype(BF16)

        def tile(kt, state, kv, vv, selected, pair=pair, kvh=kvh, qb=qb, selk=selk):
            start = pl.multiple_of(kt * tq, tq)
            dist = rel + ((qt - kt) * tq).astype(F32)
            if selected:
                expand = jnp.where(blk_row == ((start + blk_col) >> 6), 1.0, 0.0).astype(BF16)
                mask = (_dot(selk, expand) > 0.5) & (dist >= 0)
            else:
                mask = (dist >= 0) & (dist < WINDOW)
            bias = jnp.where(mask, 0.0, NEG_BIG)
            new = []
            for par, (m, l, acc) in enumerate(state):
                vi = kvh * 2 + par
                k = kv[vi, pl.ds(start, tq), :]
                v = vv[vi, pl.ds(start, tq), :]
                s = (_dot_nt(qb, k) - _head_slope(pair * 2 + par) * dist) + bias
                m_new = jnp.maximum(m, jnp.max(s, axis=1, keepdims=True))
                p = jnp.exp(s - m_new)
                alpha = jnp.exp(m - m_new)
                new.append((m_new, alpha * l + jnp.sum(p, axis=1, keepdims=True),
                            alpha * acc + _dot(p.astype(BF16), v)))
            return tuple(new)

        st_s = lax.fori_loop(0, qt + 1, lambda i, st: tile(qt - i, st, ksv, vsv, True), (init1, init1))
        st_w = lax.fori_loop(0, n_win, lambda i, st: tile(qt - i, st, kwv, vwv, False), (init1, init1))
        o_heads = []
        for par in range(2):
            h = pair * 2 + par
            g_c = sig[:, 3 * h:3 * h + 1]
            g_s = sig[:, 3 * h + 1:3 * h + 2]
            g_w = sig[:, 3 * h + 2:3 * h + 3]
            o_heads.append(g_c * oc_pair + g_s * (st_s[par][2] / st_s[par][1])
                           + g_w * (st_w[par][2] / st_w[par][1]))
        o_ref[0, :, pair * LANES:(pair + 1) * LANES] = jnp.where(lane < HEAD_DIM, o_heads[0], o_heads[1])


def _nsa_att_prompt(proj3, sel, oc):
    b, t, _ = proj3.shape
    tq = min(t, 256)
    qb = COL["b_q"] // W_GRP
    kv_blocks = [COL[k] // LANES for k in ("b_ks", "b_vs", "b_kw", "b_vw")]
    br = COL["b_br"] // LANES
    in_specs = [pl.BlockSpec((1, tq, W_GRP), lambda bi, i: (bi, i, qb))]
    in_specs += [pl.BlockSpec((1, t, LANES), functools.partial(lambda bi, i, cb: (bi, 0, cb), cb=cb))
                 for cb in kv_blocks]
    in_specs += [pl.BlockSpec((1, 2, tq, LANES), lambda bi, i: (bi, 0, i, 0)),
                 pl.BlockSpec((1, tq, LANES), lambda bi, i: (bi, i, br)),
                 pl.BlockSpec((1, tq, W_GRP), lambda bi, i: (bi, i, 0))]
    return pl.pallas_call(
        functools.partial(_nsa_att_kernel, tq=tq, t=t),
        grid=(b, t // tq),
        in_specs=in_specs,
        out_specs=pl.BlockSpec((1, tq, W_GRP), lambda bi, i: (bi, i, 0)),
        out_shape=jax.ShapeDtypeStruct((b, t, W_GRP), F32),
        scratch_shapes=[pltpu.VMEM((4, t, LANES), BF16)] * 4,
        compiler_params=_params(("parallel", "arbitrary"), 48),
        name="nsa_att_prompt",
    )(proj3, proj3, proj3, proj3, proj3, sel, proj3, oc)


def _pool_kernel(*refs, tt, start, has_halo):
    if has_halo:
        u_ref, halo_ref, pre_ref, w_ref, sc_ref, o_ref, e_ref, d_ref = refs
    else:
        u_ref, pre_ref, w_ref, sc_ref, o_ref, e_ref, d_ref = refs
    ti = pl.program_id(1)
    e_ref[POOL_HALO:POOL_HALO + tt, :] = u_ref[0]
    if has_halo:
        @pl.when(ti == 0)
        def _():
            e_ref[0:POOL_HALO, :] = pre_ref[0]

        @pl.when(ti > 0)
        def _():
            e_ref[0:POOL_HALO, :] = halo_ref[0]
    else:
        e_ref[0:POOL_HALO, :] = pre_ref[0]

    ch = min(tt, 64)
    for c in range(tt // ch):
        r0 = c * ch
        pos = start + ti * tt + r0 + _iota((ch, LANES), 0)
        for g, win in enumerate(POOL_WINDOWS):
            lanes = slice(g * LANES, (g + 1) * LANES)
            cur = e_ref[POOL_HALO + r0:POOL_HALO + r0 + ch, lanes]
            total = cur
            for s in range(1, win):
                total = total + e_ref[POOL_HALO + r0 - s:POOL_HALO + r0 - s + ch, lanes]
            cnt = jnp.minimum(win, pos + 1).astype(F32)
            d_ref[r0:r0 + ch, lanes] = total / cnt - cur
    for g in range(len(POOL_WINDOWS)):
        lanes = slice(g * LANES, (g + 1) * LANES)
        y = _dot(d_ref[:, lanes].astype(BF16), w_ref[g])
        o_ref[0, :, lanes] = y * sc_ref[:, lanes]


def _pool(u_arr, cb, prefix, w_pool, pool_scale, layer, start):
    b, t, _ = u_arr.shape
    tt = min(t, 512)
    nt = t // tt
    has_halo = nt > 1
    in_specs = [pl.BlockSpec((1, tt, W_GRP), lambda bi, i: (bi, i, cb))]
    args = [u_arr]
    if has_halo:
        per = tt // POOL_HALO
        in_specs.append(pl.BlockSpec((1, POOL_HALO, W_GRP),
                                     lambda bi, i: (bi, jnp.maximum(i * per - 1, 0), cb)))
        args.append(u_arr)
    in_specs += [pl.BlockSpec((1, POOL_HALO, W_GRP), lambda bi, i: (bi, 0, 0)),
                 pl.BlockSpec((None, 4, LANES, LANES), lambda bi, i: (layer, 0, 0, 0)),
                 pl.BlockSpec((None, 1, W_GRP), lambda bi, i: (layer, 0, 0))]
    args += [prefix, w_pool, pool_scale]
    return pl.pallas_call(
        functools.partial(_pool_kernel, tt=tt, start=start, has_halo=has_halo),
        grid=(b, nt),
        in_specs=in_specs,
        out_specs=pl.BlockSpec((1, tt, W_GRP), lambda bi, i: (bi, i, 0)),
        out_shape=jax.ShapeDtypeStruct((b, t, W_GRP), F32),
        scratch_shapes=[pltpu.VMEM((POOL_HALO + tt, W_GRP), F32), pltpu.VMEM((tt, W_GRP), F32)],
        compiler_params=_params(("parallel", "arbitrary")),
        name="pool",
    )(*args)


def _conv_kernel(*refs, tt, has_halo):
    if has_halo:
        (a_ref, g_ref, ha_ref, hg_ref, pre_ref, cw_ref, cb_ref, lg_ref, lb_ref, pw_ref,
         o_ref, u_ref, e_ref, y_ref) = refs
    else:
        (a_ref, g_ref, pre_ref, cw_ref, cb_ref, lg_ref, lb_ref, pw_ref,
         o_ref, u_ref, e_ref, y_ref) = refs
    ti = pl.program_id(1)
    u = a_ref[0] * _sigmoid(g_ref[0])
    u_ref[0] = u
    e_ref[CONV_HALO:CONV_HALO + tt, :] = u
    if has_halo:
        @pl.when(ti == 0)
        def _():
            e_ref[0:CONV_HALO, :] = pre_ref[0]

        @pl.when(ti > 0)
        def _():
            e_ref[0:CONV_HALO, :] = ha_ref[0] * _sigmoid(hg_ref[0])
    else:
        e_ref[0:CONV_HALO, :] = pre_ref[0]

    ch = min(tt, 32)
    lead = CONV_HALO - (CONV_W - 1)
    for c in range(tt // ch):
        r0 = c * ch
        acc = jnp.broadcast_to(cb_ref[...], (ch, W_GRP))
        for j in range(CONV_W):
            acc = acc + cw_ref[j:j + 1, :] * e_ref[r0 + lead + j:r0 + lead + j + ch, :]
        y_ref[r0:r0 + ch, :] = acc
    y = y_ref[...]
    yc = y - jnp.mean(y, axis=-1, keepdims=True)
    yn = yc * lax.rsqrt(jnp.mean(yc * yc, axis=-1, keepdims=True) + LN_EPS)
    yn = yn * lg_ref[...] + lb_ref[...]
    o_ref[0] = _dot((yn * _sigmoid(yn)).astype(BF16), pw_ref[...])


def _conv(arr, cb_a, cb_g, prefix, conv_w, conv_b, ln_g, ln_b, w_pw, layer):
    b, t, _ = arr.shape
    tt = min(t, 512)
    nt = t // tt
    has_halo = nt > 1
    in_specs = [pl.BlockSpec((1, tt, W_GRP), lambda bi, i: (bi, i, cb_a)),
                pl.BlockSpec((1, tt, W_GRP), lambda bi, i: (bi, i, cb_g))]
    args = [arr, arr]
    if has_halo:
        per = tt // CONV_HALO
        in_specs += [pl.BlockSpec((1, CONV_HALO, W_GRP),
                                  lambda bi, i: (bi, jnp.maximum(i * per - 1, 0), cb_a)),
                     pl.BlockSpec((1, CONV_HALO, W_GRP),
                                  lambda bi, i: (bi, jnp.maximum(i * per - 1, 0), cb_g))]
        args += [arr, arr]
    vec = lambda: pl.BlockSpec((None, 1, W_GRP), lambda bi, i: (layer, 0, 0))
    in_specs += [pl.BlockSpec((1, CONV_HALO, W_GRP), lambda bi, i: (bi, 0, 0)),
                 pl.BlockSpec((None, CONV_HALO, W_GRP), lambda bi, i: (layer, 0, 0)),
                 vec(), vec(), vec(),
                 pl.BlockSpec((None, W_GRP, W_GRP), lambda bi, i: (layer, 0, 0))]
    args += [prefix, conv_w, conv_b, ln_g, ln_b, w_pw]
    return pl.pallas_call(
        functools.partial(_conv_kernel, tt=tt, has_halo=has_halo),
        grid=(b, nt),
        in_specs=in_specs,
        out_specs=[pl.BlockSpec((1, tt, W_GRP), lambda bi, i: (bi, i, 0))] * 2,
        out_shape=[jax.ShapeDtypeStruct((b, t, W_GRP), F32)] * 2,
        scratch_shapes=[pltpu.VMEM((CONV_HALO + tt, W_GRP), F32), pltpu.VMEM((tt, W_GRP), F32)],
        compiler_params=_params(("parallel", "arbitrary")),
        name="conv",
    )(*args)


SB_ROWS = 32
PAGES_PER_STEP = 8


def _sb_sample_kernel(pt_ref, q_ref, kn_ref, vn_ref, k_hbm, v_hbm, o_ref,
                      kbuf, vbuf, sem, carry_ref, acc_ref, *, layer, n_pages, page):
    b = pl.program_id(0)
    qbd = q_ref[0].astype(BF16)
    upper = _upper_ones(page)

    def page_copies(i, slot):
        pg = pt_ref[b * n_pages + (n_pages - 1 - i)]
        return (pltpu.make_async_copy(k_hbm.at[layer, pg], kbuf.at[slot], sem.at[0, slot]),
                pltpu.make_async_copy(v_hbm.at[layer, pg], vbuf.at[slot], sem.at[1, slot]))

    def start(i, slot):
        for c in page_copies(i, slot):
            c.start()

    def wait(i, slot):
        for c in page_copies(i, slot):
            c.wait()

    start(0, 0)

    def absorb(k, v, mask, channel_major):
        kb, vb = k.astype(BF16), v.astype(BF16)
        z = _dot(qbd, kb) if channel_major else _dot_nt(qbd, kb)
        lk = _log_keep(z)
        if mask is not None:
            lk = jnp.where(mask, lk, 0.0)
        between = _split_dot(lk, upper) + carry_ref[...]
        a = jnp.exp(z + lk + between)
        if mask is not None:
            a = jnp.where(mask, a, 0.0)
        ab = a.astype(BF16)
        acc_ref[...] += _dot_nt(ab, vb) if channel_major else _dot(ab, vb)
        carry_ref[...] += jnp.sum(lk, axis=1, keepdims=True)

    carry_ref[...] = jnp.zeros_like(carry_ref)
    acc_ref[...] = jnp.zeros_like(acc_ref)
    pad = jnp.zeros((page - kn_ref.shape[1], W_GRP), F32)
    r = _iota((SB_ROWS, page), 0)
    j = _iota((SB_ROWS, page), 1)
    absorb(jnp.concatenate([kn_ref[0], pad], axis=0),
           jnp.concatenate([vn_ref[0], pad], axis=0), j < (r >> 3), False)

    def live():
        return jnp.max(carry_ref[...]) > SB_DEAD_CARRY

    def body(c):
        i = c[0]
        slot = i & 1
        wait(i, slot)

        @pl.when(i + 1 < n_pages)
        def _():
            start(i + 1, 1 - slot)

        absorb(kbuf[slot].reshape(W_GRP, page), vbuf[slot].reshape(W_GRP, page), None, True)
        return i + 1, live()

    i_end, _ = lax.while_loop(lambda c: (c[0] < n_pages) & c[1], body, (jnp.int32(0), live()))

    @pl.when(i_end < n_pages)
    def _():
        wait(i_end, i_end & 1)

    acc = acc_ref[...]
    r = _iota((SB_ROWS, HEAD_DIM), 0)
    out = jnp.zeros((SB_ROWS, HEAD_DIM), F32)
    for h in range(W_GRP // HEAD_DIM):
        out = jnp.where((r & 7) == h, acc[:, h * HEAD_DIM:(h + 1) * HEAD_DIM], out)
    o_ref[0] = out


def _sb_sample(page_table, qbd, k_new, v_new, cache_k, cache_v, layer):
    db, n_pages = page_table.shape
    n_heads, page = cache_k.shape[2], cache_k.shape[4]
    small = lambda r: pl.BlockSpec((1, r, W_GRP), lambda bi, pt: (bi, 0, 0))
    grid_spec = pltpu.PrefetchScalarGridSpec(
        num_scalar_prefetch=1,
        grid=(db,),
        in_specs=[small(SB_ROWS), small(k_new.shape[1]), small(v_new.shape[1]),
                  pl.BlockSpec(memory_space=pl.ANY), pl.BlockSpec(memory_space=pl.ANY)],
        out_specs=pl.BlockSpec((1, SB_ROWS, HEAD_DIM), lambda bi, pt: (bi, 0, 0)),
        scratch_shapes=[pltpu.VMEM((2, n_heads, HEAD_DIM, page), F32),
                        pltpu.VMEM((2, n_heads, HEAD_DIM, page), F32),
                        pltpu.SemaphoreType.DMA((2, 2)),
                        pltpu.VMEM((SB_ROWS, 1), F32), pltpu.VMEM((SB_ROWS, W_GRP), F32)],
    )
    return pl.pallas_call(
        functools.partial(_sb_sample_kernel, layer=layer, n_pages=n_pages, page=page),
        grid_spec=grid_spec,
        out_shape=jax.ShapeDtypeStruct((db, SB_ROWS, HEAD_DIM), F32),
        compiler_params=_params(("arbitrary",)),
        name="sb_sample",
    )(page_table.reshape(-1), qbd, k_new, v_new, cache_k, cache_v)


def _row_fields(shape):
    r = _iota(shape, 0)
    qi = (r >> 1) & 3
    kvh = r & 1
    h = kvh * 4 + (r >> 3)
    slope = jnp.zeros(shape, F32)
    for hh in range(8):
        slope = jnp.where(h == hh, _head_slope(hh), slope)
    return qi, kvh, slope


def _pick_half(x, kvh64):
    return jnp.where(kvh64 == 0, x[:, :HEAD_DIM], x[:, HEAD_DIM:])


def _nsa_small_kernel(pt_ref, q_ref, kall_ref, vall_ref, kwc_ref, vwc_ref, kwn_ref, vwn_ref,
                      oc_ref, ow_ref, sel_ref, used_ref, kg_ref, vg_ref, *, n_pages, past):
    b = pl.program_id(0)

    def gather(p, carry):
        idx = pt_ref[b * n_pages + p]
        kg_ref[pl.ds(p, 1), :] = kall_ref[pl.ds(idx, 1), :]
        vg_ref[pl.ds(p, 1), :] = vall_ref[pl.ds(idx, 1), :]
        return carry

    lax.fori_loop(0, n_pages, gather, 0)

    q = q_ref[0].astype(BF16)
    rows = q_ref.shape[1]
    _, kvh64, _ = _row_fields((rows, HEAD_DIM))

    qi, _, slope = _row_fields((rows, n_pages))
    pidx = _iota((rows, n_pages), 1)
    qpos = past + qi
    per_page = kg_ref.shape[1] // LANES
    scores, oks = [], []
    for n4 in range(per_page):
        z = _dot_nt(q, kg_ref[:, n4 * LANES:(n4 + 1) * LANES].astype(BF16))
        dist = (qpos - ((pidx * per_page + n4) * L_CMP + (L_CMP - 1))).astype(F32)
        ok = dist >= 0
        oks.append(ok)
        scores.append(jnp.where(ok, z - slope * dist, NEG_BIG))
    m = functools.reduce(jnp.maximum, [jnp.max(s, axis=1, keepdims=True) for s in scores])
    es = [jnp.exp(s - m) for s in scores]
    tot = functools.reduce(jnp.add, [jnp.sum(e, axis=1, keepdims=True) for e in es])
    ps = [jnp.where(ok, e / tot, 0.0) for ok, e in zip(oks, es)]
    o_c = functools.reduce(jnp.add, [
        _dot(p.astype(BF16), vg_ref[:, n4 * LANES:(n4 + 1) * LANES].astype(BF16))
        for n4, p in enumerate(ps)])
    oc_ref[0] = _pick_half(o_c, kvh64)

    def group_sum(x):
        return x[0:8] + x[8:16] + x[16:24] + x[24:32]

    lane = _iota((8, n_pages), 1)
    s_even = group_sum(ps[0] + ps[1]) + jnp.where(lane == 0, FORCE_BONUS, 0.0)
    s_odd = group_sum(ps[2] + ps[3]) + jnp.where(lane == n_pages - 1, FORCE_BONUS, 0.0)
    score = jnp.concatenate([s_even, s_odd], axis=1)
    lane2 = _iota((8, 2 * n_pages), 1)
    blk = jnp.where(lane2 < n_pages, 2 * lane2, 2 * (lane2 - n_pages) + 1)
    sel = jnp.zeros((8, 2 * n_pages), F32)
    for _ in range(N_SEL - 1):
        best = jnp.max(score, axis=1, keepdims=True)
        first = jnp.min(jnp.where(score == best, blk, 2 ** 30), axis=1, keepdims=True)
        hit = (blk == first) & (best > -jnp.inf)
        sel = jnp.where(hit, 1.0, sel)
        score = jnp.where(blk == first, -jnp.inf, score)
    sel_ref[0] = sel
    any_row = jnp.max(sel, axis=0, keepdims=True)
    used = jnp.maximum(any_row[:, :n_pages], any_row[:, n_pages:])
    used_ref[0] = jnp.broadcast_to(used, (8, n_pages)).astype(jnp.int32)

    wbuf = kwc_ref.shape[4]
    n_new = kwn_ref.shape[1]
    padw = jnp.zeros((LANES - n_new, LANES), F32)
    kw_new = jnp.concatenate([kwn_ref[0], padw], axis=0).astype(BF16)
    vw_new = jnp.concatenate([vwn_ref[0], padw], axis=0).astype(BF16)
    kw_old = kwc_ref[0, 0].reshape(LANES, wbuf).astype(BF16)
    vw_old = vwc_ref[0, 0].reshape(LANES, wbuf).astype(BF16)

    def scores(z, dist):
        ok = (dist >= 0) & (dist < WINDOW)
        return jnp.where(ok, z - slope_of(dist.shape) * dist.astype(F32), NEG_BIG), ok

    def slope_of(shape):
        return _row_fields(shape)[2]

    qi_o, _, _ = _row_fields((rows, wbuf))
    qi_n, _, _ = _row_fields((rows, LANES))
    s_o, ok_o = scores(_dot(q, kw_old), wbuf + qi_o - _iota((rows, wbuf), 1))
    s_n, ok_n = scores(_dot_nt(q, kw_new), qi_n - _iota((rows, LANES), 1))
    m = jnp.maximum(jnp.max(s_o, axis=1, keepdims=True), jnp.max(s_n, axis=1, keepdims=True))
    e_o, e_n = jnp.exp(s_o - m), jnp.exp(s_n - m)
    tot = jnp.sum(e_o, axis=1, keepdims=True) + jnp.sum(e_n, axis=1, keepdims=True)
    p_o = jnp.where(ok_o, e_o / tot, 0.0).astype(BF16)
    p_n = jnp.where(ok_n, e_n / tot, 0.0).astype(BF16)
    ow_ref[0] = _pick_half(_dot_nt(p_o, vw_old) + _dot(p_n, vw_new), kvh64)


def _nsa_small(page_table, q_rows, kall, vall, win_k, win_v, kw_new, vw_new, layer, past):
    db, n_pages = page_table.shape
    rows = q_rows.shape[1]
    wbuf = win_k.shape[4]
    whole = lambda a: pl.BlockSpec(a.shape, lambda bi, pt: (0,) * a.ndim)
    per_b = lambda a: pl.BlockSpec((1,) + a.shape[1:], lambda bi, pt: (bi,) + (0,) * (a.ndim - 1))
    win = pl.BlockSpec((1, 1, 2, HEAD_DIM, wbuf), lambda bi, pt: (layer, bi, 0, 0, 0))
    grid_spec = pltpu.PrefetchScalarGridSpec(
        num_scalar_prefetch=1,
        grid=(db,),
        in_specs=[per_b(q_rows), whole(kall), whole(vall), win, win, per_b(kw_new), per_b(vw_new)],
        out_specs=[pl.BlockSpec((1, rows, HEAD_DIM), lambda bi, pt: (bi, 0, 0)),
                   pl.BlockSpec((1, rows, HEAD_DIM), lambda bi, pt: (bi, 0, 0)),
                   pl.BlockSpec((1, 8, 2 * n_pages), lambda bi, pt: (bi, 0, 0)),
                   pl.BlockSpec((1, 8, n_pages), lambda bi, pt: (bi, 0, 0))],
        scratch_shapes=[pltpu.VMEM((n_pages, kall.shape[1]), F32)] * 2,
    )
    return pl.pallas_call(
        functools.partial(_nsa_small_kernel, n_pages=n_pages, past=past),
        grid_spec=grid_spec,
        out_shape=[jax.ShapeDtypeStruct((db, rows, HEAD_DIM), F32),
                   jax.ShapeDtypeStruct((db, rows, HEAD_DIM), F32),
                   jax.ShapeDtypeStruct((db, 8, 2 * n_pages), F32),
                   jax.ShapeDtypeStruct((db, 8, n_pages), jnp.int32)],
        compiler_params=_params(("arbitrary",)),
        name="nsa_small_sample",
    )(page_table.reshape(-1), q_rows, kall, vall, win_k, win_v, kw_new, vw_new)


def _nsa_sel_kernel(pt_ref, used_ref, q_ref, sel_ref, kn_ref, vn_ref, oc_ref, ow_ref, br_ref, *rest,
                    npg, n_pages, page, past):
    del pt_ref
    krefs, vrefs = rest[:npg], rest[npg:2 * npg]
    o_ref, m_ref, l_ref, acc_ref = rest[2 * npg:]
    b = pl.program_id(0)
    step = pl.program_id(1)
    q = q_ref[0].astype(BF16)
    rows = q_ref.shape[1]
    qi, _, slope = _row_fields((rows, page))
    lane = _iota((rows, page), 1)
    qpos = past + qi

    def absorb(k, v, kpos0, mask, channel_major):
        kb, vb = k.astype(BF16), v.astype(BF16)
        dist = (qpos - (kpos0 + lane)).astype(F32)
        s = (_dot(q, kb) if channel_major else _dot_nt(q, kb)) - slope * dist
        if mask is None:
            mask = dist >= 0
        m_old = m_ref[...]
        m_new = jnp.maximum(m_old, jnp.max(jnp.where(mask, s, NEG_BIG), axis=1, keepdims=True))
        p = jnp.where(mask, jnp.exp(s - m_new), 0.0)
        alpha = jnp.exp(m_old - m_new)
        l_ref[...] = alpha * l_ref[...] + jnp.sum(p, axis=1, keepdims=True)
        pb = p.astype(BF16)
        acc_ref[...] = alpha * acc_ref[...] + (_dot_nt(pb, vb) if channel_major else _dot(pb, vb))
        m_ref[...] = m_new

    @pl.when(step == 0)
    def _():
        m_ref[...] = jnp.full_like(m_ref, NEG_BIG)
        l_ref[...] = jnp.zeros_like(l_ref)
        acc_ref[...] = jnp.zeros_like(acc_ref)
        pad = jnp.zeros((page - kn_ref.shape[1], LANES), F32)
        absorb(jnp.concatenate([kn_ref[0], pad], axis=0),
               jnp.concatenate([vn_ref[0], pad], axis=0), past, None, False)

    selm = jnp.concatenate([sel_ref[0]] * (rows // 8), axis=0)
    pl_lane = _iota((rows, n_pages), 1)
    per_blk = page // L_SEL
    for g in range(npg):
        pg = n_pages - 1 - (step * npg + g)

        @pl.when(used_ref[b * n_pages + pg] > 0)
        def _(g=g, pg=pg):
            mask = jnp.zeros((rows, page), jnp.bool_)
            for j in range(per_blk):
                cj = jnp.sum(jnp.where(pl_lane == pg, selm[:, j * n_pages:(j + 1) * n_pages], 0.0),
                             axis=1, keepdims=True)
                mask = mask | ((cj > 0.5) & (lane >= j * L_SEL) & (lane < (j + 1) * L_SEL))
            absorb(krefs[g][0, 0].reshape(LANES, page), vrefs[g][0, 0].reshape(LANES, page),
                   pg * page, mask, True)

    @pl.when(step == pl.num_programs(1) - 1)
    def _():
        _, kvh64, _ = _row_fields((rows, HEAD_DIM))
        o_s = _pick_half(acc_ref[...] / l_ref[...], kvh64)
        sig = _sigmoid(br_ref[0])
        o_ref[0] = sig[:, 0:1] * oc_ref[0] + sig[:, 1:2] * o_s + sig[:, 2:3] * ow_ref[0]


def _nsa_sel_sample(page_table, used, q_rows, sel, k_new, v_new, oc, ow, br_rows, cache_k, cache_v,
                    layer, past):
    db, n_pages = page_table.shape
    rows = q_rows.shape[1]
    page = cache_k.shape[4]
    npg = min(PAGES_PER_STEP, n_pages)
    steps = n_pages // npg

    def page_spec(g):
        def index(bi, s, pt, us):
            return (layer, pt[bi * n_pages + (n_pages - 1 - (s * npg + g))], 0, 0, 0)
        return pl.BlockSpec((1, 1, 2, HEAD_DIM, page), index)

    per_b = lambda a: pl.BlockSpec((1,) + a.shape[1:],
                                   lambda bi, s, pt, us: (bi,) + (0,) * (a.ndim - 1))
    grid_spec = pltpu.PrefetchScalarGridSpec(
        num_scalar_prefetch=2,
        grid=(db, steps),
        in_specs=[per_b(a) for a in (q_rows, sel, k_new, v_new, oc, ow, br_rows)]
        + [page_spec(g) for g in range(npg)] * 2,
        out_specs=pl.BlockSpec((1, rows, HEAD_DIM), lambda bi, s, pt, us: (bi, 0, 0)),
        scratch_shapes=[pltpu.VMEM((rows, 1), F32), pltpu.VMEM((rows, 1), F32),
                        pltpu.VMEM((rows, LANES), F32)],
    )
    return pl.pallas_call(
        functools.partial(_nsa_sel_kernel, npg=npg, n_pages=n_pages, page=page, past=past),
        grid_spec=grid_spec,
        out_shape=jax.ShapeDtypeStruct((db, rows, HEAD_DIM), F32),
        compiler_params=_params(("parallel", "arbitrary")),
        name="nsa_sel_sample",
    )(page_table.reshape(-1), used.reshape(-1), q_rows, sel, k_new, v_new, oc, ow, br_rows,
      *([cache_k] * npg), *([cache_v] * npg))


def _pack_w_in(w_in):
    depth, d, _ = w_in.shape
    pieces, src = [], 0
    placed = {}
    for name, width in _SRC:
        placed[name] = (src, width)
        src += width
    order = sorted(COL, key=COL.get)
    pos = 0
    for name in order:
        assert COL[name] == pos
        s, width = placed[name]
        pieces.append(w_in[:, :, s:s + width])
        pos += width
    pieces.append(jnp.zeros((depth, d, N_PACK - pos), w_in.dtype))
    return jnp.concatenate(pieces, axis=2).astype(BF16)


def _expand_cmp_weight(w):
    depth, l, d, e = w.shape
    eye = jnp.eye(2, dtype=w.dtype)
    return jnp.einsum("zlde,hg->zlhdge", w, eye).reshape(depth, l * 2 * d, 2 * e).astype(BF16)


def _expand_cmp_weight_t(w, page):
    depth, l, d, e = w.shape
    n = page // l
    eye = jnp.eye(n, dtype=w.dtype)
    return jnp.einsum("zlde,nm->zdnlme", w, eye).reshape(depth, d, page, n * e).astype(BF16)


def _sample_rows(x, n_heads_last):
    db, tq = x.shape[:2]
    x = x.reshape(db, tq, 2, 4, n_heads_last)
    return x.transpose(0, 3, 1, 2, 4).reshape(db, 4 * tq * 2, n_heads_last)


def kernel(x_prompt, x_sample, cache_sb_k, cache_sb_v, cache_cmp_k, cache_cmp_v, cache_sel_k, cache_sel_v, cache_win_k, cache_win_v, state_pool, state_conv, page_table, norm_g, w_in, w_cmp_k, w_cmp_v, w_pool, pool_scale, conv_w, conv_b, ln_g, ln_b, w_pw, w_out, final_g):
    bp, seq, d_model = x_prompt.shape
    db, tq_s, _ = x_sample.shape
    depth, n_pool, page = cache_sb_k.shape[:3]
    assert tq_s == 4 and page == LANES and cache_win_k.shape[2] == WINDOW and seq % 256 == 0

    w = _prepare_weights(norm_g, w_in, w_cmp_k, w_cmp_v, w_pool, pool_scale, conv_w, conv_b,
                         ln_g, ln_b, w_pw, w_out, final_g)
    chan_major = lambda c: jnp.transpose(c, (0, 1, 3, 4, 2))
    sbk, sbv = chan_major(cache_sb_k), chan_major(cache_sb_v)
    selk, selv = chan_major(cache_sel_k), chan_major(cache_sel_v)
    cmpk = chan_major(cache_cmp_k).reshape(depth * n_pool * 2 * HEAD_DIM, page)
    cmpv = chan_major(cache_cmp_v).reshape(depth * n_pool * 2 * HEAD_DIM, page)

    hp = x_prompt.reshape(bp * seq, d_model)
    hs = x_sample.reshape(db * tq_s, d_model)
    p_states, s_states = [], []
    for l in range(depth):
        last = l == depth - 1
        hp, st = _prompt_layer(hp, bp, seq, w, l, last)
        p_states.append(st)
        hs, st = _sample_layer(hs, db, tq_s, w, l, last, page_table, sbk, sbv, cmpk, cmpv, selk, selv,
                               cache_win_k, cache_win_v, state_pool, state_conv)
        s_states.append(st)

    stacked = lambda states, j: jnp.stack([st[j] for st in states], axis=0)
    return (hp.reshape(bp, seq, d_model), hs.reshape(db, tq_s, d_model),
            *[stacked(p_states, j) for j in range(10)],
            *[stacked(s_states, j) for j in range(10)])


def _prepare_weights(norm_g, w_in, w_cmp_k, w_cmp_v, w_pool, pool_scale, conv_w, conv_b, ln_g, ln_b,
                     w_pw, w_out, final_g):
    depth = w_in.shape[0]
    vec3 = lambda a: a.reshape(depth, 1, -1)
    return dict(
        w_pack=_pack_w_in(w_in), w_out=w_out.astype(BF16),
        wk_exp=_expand_cmp_weight(w_cmp_k), wv_exp=_expand_cmp_weight(w_cmp_v),
        wk_t=_expand_cmp_weight_t(w_cmp_k, LANES), wv_t=_expand_cmp_weight_t(w_cmp_v, LANES),
        w_pool=w_pool.astype(BF16), w_pw=w_pw.astype(BF16),
        conv_w=jnp.pad(conv_w, ((0, 0), (0, CONV_HALO - CONV_W), (0, 0))),
        norm_g=vec3(norm_g), pool_scale=vec3(pool_scale), conv_b=vec3(conv_b),
        ln_g=vec3(ln_g), ln_b=vec3(ln_b), final_g=final_g.reshape(1, -1))


def _prompt_layer(hp, bp, seq, w, l, last):
    proj = _inproj(hp, w["norm_g"], w["w_pack"], l)
    proj3 = proj.reshape(bp, seq, N_PACK)
    col = lambda name, width: proj3[:, :, COL[name]:COL[name] + width]
    kc, vc = col("b_kc", LANES), col("b_vc", LANES)
    o_a = _sb_prompt(proj3)
    n_blocks = bp * seq // L_CMP
    kcmp, vcmp = _compress(kc.reshape(n_blocks, L_CMP * LANES), vc.reshape(n_blocks, L_CMP * LANES),
                           w["wk_exp"], w["wv_exp"], l, 0, n_blocks)
    n_cmp = seq // L_CMP
    o_cmp, sel = _nsa_cmp_prompt(proj3, kcmp.reshape(bp, n_cmp, LANES), vcmp.reshape(bp, n_cmp, LANES))
    o_b = _nsa_att_prompt(proj3, sel, o_cmp)
    zero_pool = jnp.zeros((bp, POOL_HALO, W_GRP), F32)
    zero_conv = jnp.zeros((bp, CONV_HALO, W_GRP), F32)
    o_c = _pool(proj3, COL["c_u"] // W_GRP, zero_pool, w["w_pool"], w["pool_scale"], l, 0)
    o_d, u_conv = _conv(proj3, COL["d_a"] // W_GRP, COL["d_gl"] // W_GRP, zero_conv,
                        w["conv_w"], w["conv_b"], w["ln_g"], w["ln_b"], w["w_pw"], l)
    flat = lambda a: a.reshape(bp * seq, W_GRP)
    hp_new = _outproj([flat(o_a), flat(o_b), flat(o_c), flat(o_d)], proj, hp, w["w_out"], l,
                      w["final_g"] if last else None)
    keep = min(WINDOW, seq)
    heads = lambda a, n: a.reshape(bp, -1, n, HEAD_DIM)
    states = (
        heads(col("a_k", W_GRP), 8), heads(col("a_v", W_GRP), 8),
        heads(kc, 2), heads(vc, 2),
        heads(col("b_ks", LANES), 2), heads(col("b_vs", LANES), 2),
        heads(col("b_kw", LANES)[:, seq - keep:], 2), heads(col("b_vw", LANES)[:, seq - keep:], 2),
        col("c_u", W_GRP)[:, seq - (POOL_HALO - 1):], u_conv[:, seq - (CONV_W - 1):])
    return hp_new, states


def _sample_layer(hs, db, tq_s, w, l, last, page_table, sbk, sbv, cmpk, cmpv, selk, selv,
                  cache_win_k, cache_win_v, state_pool, state_conv):
    depth, n_pool = sbk.shape[:2]
    page = sbk.shape[4]
    n_pages = page_table.shape[1]
    past = n_pages * page
    projs = _inproj(hs, w["norm_g"], w["w_pack"], l)
    projs3 = projs.reshape(db, tq_s, N_PACK)
    scol = lambda name, width: projs3[:, :, COL[name]:COL[name] + width]
    pad8 = lambda a: jnp.pad(a, ((0, 0), (0, 8 - tq_s), (0, 0)))
    head_of_col = jnp.arange(W_GRP) // HEAD_DIM
    sb_mask = (head_of_col[None, :] == jnp.arange(8)[:, None]).astype(F32)
    a_q = scol("a_q", W_GRP) * SCALE
    qbd = (a_q[:, :, None, :] * sb_mask[None, None]).reshape(db, tq_s * 8, W_GRP)
    o_a = _sb_sample(page_table, qbd, pad8(scol("a_k", W_GRP)), pad8(scol("a_v", W_GRP)), sbk, sbv, l)
    o_a = o_a.reshape(db * tq_s, W_GRP)
    kall, vall = _compress_cache(cmpk, cmpv, w["wk_t"], w["wv_t"], l, n_pool * 2)
    per_page = page // L_CMP
    relane = lambda a: a.reshape(n_pool, 2, per_page, HEAD_DIM).transpose(0, 2, 1, 3).reshape(
        n_pool, per_page * LANES)
    kall, vall = relane(kall), relane(vall)
    b_q = _sample_rows(scol("b_q", W_GRP) * SCALE, HEAD_DIM)
    kvh_of_row = jnp.arange(b_q.shape[1]) & 1
    q_rows = (b_q[:, :, None, :] * jnp.eye(2, dtype=F32)[kvh_of_row][None, :, :, None]).reshape(db, -1, LANES)
    br_rows = jnp.pad(_sample_rows(scol("b_br", 24), 3), ((0, 0), (0, 0), (0, LANES - 3)))
    kw_new, vw_new = scol("b_kw", LANES), scol("b_vw", LANES)
    wink = jnp.transpose(cache_win_k, (0, 1, 3, 4, 2))
    winv = jnp.transpose(cache_win_v, (0, 1, 3, 4, 2))
    o_cmp, o_win, sel, used = _nsa_small(page_table, q_rows, kall, vall, wink, winv,
                                         pad8(kw_new), pad8(vw_new), l, past)
    o_b_rows = _nsa_sel_sample(page_table, used[:, 0], q_rows, sel, pad8(scol("b_ks", LANES)),
                               pad8(scol("b_vs", LANES)), o_cmp, o_win, br_rows, selk, selv, l, past)
    o_b = o_b_rows.reshape(db, 4, tq_s, 2, HEAD_DIM).transpose(0, 2, 3, 1, 4).reshape(db * tq_s, W_GRP)
    pool_pre = jnp.pad(state_pool[l], ((0, 0), (POOL_HALO - state_pool.shape[2], 0), (0, 0)))
    o_c = _pool(pad8(scol("c_u", W_GRP)), 0, pool_pre, w["w_pool"], w["pool_scale"], l, past)
    o_c = o_c[:, :tq_s].reshape(db * tq_s, W_GRP)
    conv_pre = jnp.pad(state_conv[l], ((0, 0), (CONV_HALO - state_conv.shape[2], 0), (0, 0)))
    o_d, u_conv = _conv(pad8(scol("d_a", 2 * W_GRP)), 0, 1, conv_pre,
                        w["conv_w"], w["conv_b"], w["ln_g"], w["ln_b"], w["w_pw"], l)
    o_d = o_d[:, :tq_s].reshape(db * tq_s, W_GRP)
    hs_new = _outproj([o_a, o_b, o_c, o_d], projs, hs, w["w_out"], l, w["final_g"] if last else None)
    sheads = lambda a, n: a.reshape(db, -1, n, HEAD_DIM)
    states = (
        sheads(scol("a_k", W_GRP), 8), sheads(scol("a_v", W_GRP), 8),
        sheads(scol("b_kc", LANES), 2), sheads(scol("b_vc", LANES), 2),
        sheads(scol("b_ks", LANES), 2), sheads(scol("b_vs", LANES), 2),
        jnp.concatenate([cache_win_k[l][:, tq_s:], sheads(kw_new, 2)], axis=1),
        jnp.concatenate([cache_win_v[l][:, tq_s:], sheads(vw_new, 2)], axis=1),
        jnp.concatenate([state_pool[l][:, tq_s:], scol("c_u", W_GRP)], axis=1),
        jnp.concatenate([state_conv[l][:, tq_s:], u_conv[:, :tq_s]], axis=1))
    return hs_new, states
```

```python
import functools

import jax
import jax.numpy as jnp
from jax import lax
from jax.experimental import pallas as pl
from jax.experimental.pallas import tpu as pltpu

F32 = jnp.float32
BF16 = jnp.bfloat16

HEAD_DIM = 64
LANES = 128
W_GRP = 512
L_CMP = 32
L_SEL = 64
N_SEL = 16
WINDOW = 512
FORCE_BONUS = 1000.0
POOL_WINDOWS = (2, 4, 8, 16)
POOL_HALO = 16
CONV_W = 31
CONV_HALO = 32
RMS_EPS = 1e-6
LN_EPS = 1e-5
NEG_BIG = -1e30
SCALE = HEAD_DIM ** -0.5
SB_DEAD_CARRY = -104.0

COL = dict(a_q=0, a_k=512, a_v=1024, a_g=1536, b_q=2048, b_g=2560, c_u=3072, c_g=3584,
           d_a=4096, d_gl=4608, d_g=5120, b_kc=5632, b_vc=5760, b_ks=5888, b_vs=6016,
           b_kw=6144, b_vw=6272, b_br=6400)
N_PACK = 6656
_SRC = (("a_q", 512), ("a_k", 512), ("a_v", 512), ("a_g", 512), ("b_q", 512), ("b_kc", 128),
        ("b_vc", 128), ("b_ks", 128), ("b_vs", 128), ("b_kw", 128), ("b_vw", 128), ("b_g", 512),
        ("b_br", 24), ("c_u", 512), ("c_g", 512), ("d_a", 512), ("d_gl", 512), ("d_g", 512))


def _params(sem, vmem_mb=None):
    kw = dict(dimension_semantics=sem)
    if vmem_mb is not None:
        kw["vmem_limit_bytes"] = vmem_mb * 2 ** 20
    return pltpu.CompilerParams(**kw)


def _sigmoid(x):
    return 1.0 / (1.0 + jnp.exp(-x))


def _iota(shape, axis):
    return lax.broadcasted_iota(jnp.int32, shape, axis)


def _dot_nt(a, b):
    return lax.dot_general(a, b, (((1,), (1,)), ((), ())), preferred_element_type=F32)


def _dot(a, b):
    return jnp.dot(a, b, preferred_element_type=F32)


def _split_dot(x, m):
    hi = x.astype(BF16)
    lo = (x - hi.astype(F32)).astype(BF16)
    return _dot(hi, m) + _dot(lo, m)


def _log_keep(z):
    return jnp.minimum(-z, 0.0) - jnp.log(1.0 + jnp.exp(-jnp.abs(z)))


def _upper_ones(n):
    return jnp.where(_iota((n, n), 0) > _iota((n, n), 1), 1.0, 0.0).astype(BF16)


def _inproj_kernel(x_ref, g_ref, w_ref, o_ref, hn_ref):
    @pl.when(pl.program_id(1) == 0)
    def _():
        x = x_ref[...]
        ms = jnp.mean(x * x, axis=-1, keepdims=True)
        hn_ref[...] = ((x * lax.rsqrt(ms + RMS_EPS)) * g_ref[...]).astype(BF16)

    o_ref[...] = _dot_nt(hn_ref[...], w_ref[...])


def _inproj(x, norm_g, w_pack, layer):
    rows, d = x.shape
    n = w_pack.shape[1]
    tm = min(rows, 1024)
    tn = n // 4 if (n // 4) % LANES == 0 else 512
    return pl.pallas_call(
        _inproj_kernel,
        grid=(rows // tm, n // tn),
        in_specs=[pl.BlockSpec((tm, d), lambda i, j: (i, 0)),
                  pl.BlockSpec((None, 1, d), lambda i, j: (layer, 0, 0)),
                  pl.BlockSpec((None, tn, d), lambda i, j: (layer, j, 0))],
        out_specs=pl.BlockSpec((tm, tn), lambda i, j: (i, j)),
        out_shape=jax.ShapeDtypeStruct((rows, n), F32),
        scratch_shapes=[pltpu.VMEM((tm, d), BF16)],
        compiler_params=_params(("parallel", "arbitrary"), 56),
        name="inproj",
    )(x, norm_g, w_pack)


def _outproj_kernel(oa, ob, oc, od, ga, gb, gc, gd, h_ref, w_ref, *rest, final):
    if final:
        fg_ref, o_ref = rest
    else:
        (o_ref,) = rest
    acc = h_ref[...]
    for i, (o, g) in enumerate(((oa, ga), (ob, gb), (oc, gc), (od, gd))):
        gv = g[...]
        mix = (o[...] * (gv * _sigmoid(gv))).astype(BF16)
        acc = acc + _dot(mix, w_ref[i * W_GRP:(i + 1) * W_GRP, :])
    if final:
        ms = jnp.mean(acc * acc, axis=-1, keepdims=True)
        acc = (acc * lax.rsqrt(ms + RMS_EPS)) * fg_ref[...]
    o_ref[...] = acc


def _outproj(outs, proj, h, w_out, layer, final_g=None):
    rows, d = h.shape
    tm = min(rows, 256)
    final = final_g is not None
    gate_blocks = [COL[k] // W_GRP for k in ("a_g", "b_g", "c_g", "d_g")]
    in_specs = [pl.BlockSpec((tm, W_GRP), lambda i: (i, 0)) for _ in range(4)]
    in_specs += [pl.BlockSpec((tm, W_GRP), functools.partial(lambda i, cb: (i, cb), cb=cb))
                 for cb in gate_blocks]
    in_specs += [pl.BlockSpec((tm, d), lambda i: (i, 0)),
                 pl.BlockSpec((None, 4 * W_GRP, d), lambda i: (layer, 0, 0))]
    args = list(outs) + [proj] * 4 + [h, w_out]
    if final:
        in_specs.append(pl.BlockSpec((1, d), lambda i: (0, 0)))
        args.append(final_g)
    return pl.pallas_call(
        functools.partial(_outproj_kernel, final=final),
        grid=(rows // tm,),
        in_specs=in_specs,
        out_specs=pl.BlockSpec((tm, d), lambda i: (i, 0)),
        out_shape=jax.ShapeDtypeStruct((rows, d), F32),
        compiler_params=_params(("parallel",), 48),
        name="outproj",
    )(*args)


SB_PAIRS = 2


def _sb_prompt_kernel(q_ref, k_ref, v_ref, o_ref, *, tq):
    qt = pl.program_id(2)
    lane = _iota((tq, LANES), 1)
    row = _iota((tq, tq), 0)
    col = _iota((tq, tq), 1)
    before = col < row
    upper = _upper_ones(tq)
    q = q_ref[0] * SCALE
    chains = []
    for pair in range(SB_PAIRS):
        qp = q[:, pair * LANES:(pair + 1) * LANES]
        for hh in range(2):
            in_head = (lane >= hh * HEAD_DIM) & (lane < (hh + 1) * HEAD_DIM)
            chains.append((pair, jnp.where(in_head, qp, 0.0).astype(BF16)))

    def tile(kt, state, diag):
        start = pl.multiple_of(kt * tq, tq)
        kv = [(k_ref[0, pl.ds(start, tq), p * LANES:(p + 1) * LANES].astype(BF16),
               v_ref[0, pl.ds(start, tq), p * LANES:(p + 1) * LANES].astype(BF16))
              for p in range(SB_PAIRS)]
        new = []
        for (pair, qm), (carry, acc) in zip(chains, state):
            k, v = kv[pair]
            z = _dot_nt(qm, k)
            lk = _log_keep(z)
            if diag:
                lk = jnp.where(before, lk, 0.0)
            between = _split_dot(lk, upper) + carry
            a = jnp.exp(z + lk + between)
            if diag:
                a = jnp.where(before, a, 0.0)
            new.append((carry + jnp.sum(lk, axis=1, keepdims=True), acc + _dot(a.astype(BF16), v)))
        return tuple(new)

    def any_live(state):
        top = functools.reduce(jnp.maximum, [jnp.max(carry) for carry, _ in state])
        return top > SB_DEAD_CARRY

    zero = (jnp.zeros((tq, 1), F32), jnp.zeros((tq, LANES), F32))
    state = tile(qt, (zero,) * len(chains), True)

    def cond(c):
        return (c[0] < qt) & c[1]

    def body(c):
        st = tile(qt - 1 - c[0], c[2], False)
        return c[0] + 1, any_live(st), st

    _, _, state = lax.while_loop(cond, body, (jnp.int32(0), any_live(state), state))
    for pair in range(SB_PAIRS):
        o_ref[0, :, pair * LANES:(pair + 1) * LANES] = jnp.where(
            lane < HEAD_DIM, state[2 * pair][1], state[2 * pair + 1][1])


def _sb_prompt(proj3):
    b, t, _ = proj3.shape
    tq = min(t, 256)
    width = SB_PAIRS * LANES
    qb, kb, vb = (COL[k] // width for k in ("a_q", "a_k", "a_v"))
    return pl.pallas_call(
        functools.partial(_sb_prompt_kernel, tq=tq),
        grid=(b, W_GRP // width, t // tq),
        in_specs=[pl.BlockSpec((1, tq, width), lambda bi, p, i: (bi, i, qb + p)),
                  pl.BlockSpec((1, t, width), lambda bi, p, i: (bi, 0, kb + p)),
                  pl.BlockSpec((1, t, width), lambda bi, p, i: (bi, 0, vb + p))],
        out_specs=pl.BlockSpec((1, tq, width), lambda bi, p, i: (bi, i, p)),
        out_shape=jax.ShapeDtypeStruct((b, t, W_GRP), F32),
        compiler_params=_params(("parallel", "parallel", "arbitrary")),
        name="sb_prompt",
    )(proj3, proj3, proj3)


def _cmp_kernel(xk_ref, xv_ref, wk_ref, wv_ref, ok_ref, ov_ref):
    ok_ref[...] = _dot(xk_ref[...].astype(BF16), wk_ref[...])
    ov_ref[...] = _dot(xv_ref[...].astype(BF16), wv_ref[...])


def _compress(xk, xv, wk_exp, wv_exp, layer, row0, rows):
    kdim = xk.shape[1]
    tr = min(rows, 256)
    off = row0 // tr
    return pl.pallas_call(
        _cmp_kernel,
        grid=(rows // tr,),
        in_specs=[pl.BlockSpec((tr, kdim), lambda i: (off + i, 0)),
                  pl.BlockSpec((tr, kdim), lambda i: (off + i, 0)),
                  pl.BlockSpec((None, kdim, LANES), lambda i: (layer, 0, 0)),
                  pl.BlockSpec((None, kdim, LANES), lambda i: (layer, 0, 0))],
        out_specs=[pl.BlockSpec((tr, LANES), lambda i: (i, 0))] * 2,
        out_shape=[jax.ShapeDtypeStruct((rows, LANES), F32)] * 2,
        compiler_params=_params(("parallel",), 48),
        name="compress",
    )(xk, xv, wk_exp, wv_exp)


def _cmp_cache_kernel(x_ref, w_ref, o_ref, *, rows):
    acc = jnp.zeros((rows, o_ref.shape[1]), F32)
    for d in range(HEAD_DIM):
        acc = acc + _dot(x_ref[pl.ds(d, rows, stride=HEAD_DIM), :].astype(BF16), w_ref[d])
    o_ref[...] = acc


CMP_CACHE_ROWS = 512


def _compress_cache(x, w_t, layer, n_rows):
    page = x.shape[1]
    n_out = w_t.shape[3]
    tr = CMP_CACHE_ROWS if n_rows % CMP_CACHE_ROWS == 0 else n_rows
    off = layer * (n_rows // tr)
    return pl.pallas_call(
        functools.partial(_cmp_cache_kernel, rows=tr),
        grid=(n_rows // tr,),
        in_specs=[pl.BlockSpec((tr * HEAD_DIM, page), lambda i: (off + i, 0)),
                  pl.BlockSpec((None, HEAD_DIM, page, n_out), lambda i: (layer, 0, 0, 0))],
        out_specs=pl.BlockSpec((tr, n_out), lambda i: (i, 0)),
        out_shape=jax.ShapeDtypeStruct((n_rows, n_out), F32),
        compiler_params=_params(("parallel",), 56),
        name="compress_cache",
    )(x, w_t)


def _head_slope(h):
    return 2.0 ** -(h + 1)


def _half_variant(x, x_swapped, kvh, par, low):
    src = x if kvh == par else x_swapped
    keep = low if par == 0 else jnp.logical_not(low)
    return jnp.where(keep, src, 0.0).astype(BF16)


def _nsa_cmp_kernel(q_ref, kc_ref, vc_ref, oc_ref, sel_ref, *, tq, ncmp):
    qt = pl.program_id(1)
    nsel = ncmp // 2
    ntop = min(N_SEL, nsel)
    half = LANES // 2

    def permuted(ref):
        ev = ref[0, pl.ds(0, nsel, stride=2), :]
        od = ref[0, pl.ds(1, nsel, stride=2), :]
        if nsel == half:
            return jnp.concatenate([ev, od], axis=0)
        pad = jnp.zeros((half - nsel, LANES), F32)
        return jnp.concatenate([ev, pad, od, pad], axis=0)

    kc = permuted(kc_ref)
    vc = permuted(vc_ref)
    kc_sw = pltpu.roll(kc, half, 1)
    vc_sw = pltpu.roll(vc, half, 1)
    low = _iota((LANES, LANES), 1) < half

    n_i = _iota((tq, LANES), 1)
    odd = n_i >= half
    slot = jnp.where(odd, n_i - half, n_i)
    cmp_blk = jnp.where(odd, 2 * slot + 1, 2 * slot)
    qpos = qt * tq + _iota((tq, LANES), 0)
    dist = (qpos - (cmp_blk * L_CMP + (L_CMP - 1))).astype(F32)
    ok = (dist >= 0) & (slot < nsel)

    q = q_ref[0] * SCALE
    o_pairs = [jnp.zeros((tq, LANES), F32) for _ in range(4)]
    for kvh in range(2):
        imp = jnp.zeros((tq, LANES), F32)
        for g in range(4):
            h = kvh * 4 + g
            par, pair = h % 2, h // 2
            qb = q[:, pair * LANES:(pair + 1) * LANES].astype(BF16)
            z = _dot_nt(qb, _half_variant(kc, kc_sw, kvh, par, low))
            s = jnp.where(ok, z - _head_slope(h) * dist, NEG_BIG)
            e = jnp.exp(s - jnp.max(s, axis=1, keepdims=True))
            p = jnp.where(ok, e / jnp.sum(e, axis=1, keepdims=True), 0.0)
            imp = imp + p
            o_pairs[pair] = o_pairs[pair] + _dot(p.astype(BF16), _half_variant(vc, vc_sw, kvh, par, low))
        imp_sel = imp + pltpu.roll(imp, half, 1)
        blk = n_i
        cur = qpos >> 6
        forced = (blk == 0) | (blk == cur) | (blk == cur - 1)
        valid = (blk * L_SEL <= qpos) & (blk < nsel)
        score = jnp.where(valid, imp_sel + jnp.where(forced, FORCE_BONUS, 0.0), -jnp.inf)
        def ranked(score=score, valid=valid):
            rank = jnp.zeros((tq, LANES), F32)
            for i in range(nsel):
                si = score[:, i:i + 1]
                beats = (si > score) | ((si == score) & (blk > i))
                rank = rank + jnp.where(beats, 1.0, 0.0)
            return jnp.where(valid & (rank < ntop), 1.0, 0.0)

        sel_ref[0, kvh] = lax.cond((qt + 1) * tq > ntop * L_SEL, ranked,
                                   lambda valid=valid: jnp.where(valid, 1.0, 0.0))
    oc_ref[0] = jnp.concatenate(o_pairs, axis=1)


def _nsa_cmp_prompt(proj3, kcmp, vcmp):
    b, t, _ = proj3.shape
    ncmp = kcmp.shape[1]
    tq = min(t, 256)
    qb = COL["b_q"] // W_GRP
    return pl.pallas_call(
        functools.partial(_nsa_cmp_kernel, tq=tq, ncmp=ncmp),
        grid=(b, t // tq),
        in_specs=[pl.BlockSpec((1, tq, W_GRP), lambda bi, i: (bi, i, qb)),
                  pl.BlockSpec((1, ncmp, LANES), lambda bi, i: (bi, 0, 0)),
                  pl.BlockSpec((1, ncmp, LANES), lambda bi, i: (bi, 0, 0))],
        out_specs=[pl.BlockSpec((1, tq, W_GRP), lambda bi, i: (bi, i, 0)),
                   pl.BlockSpec((1, 2, tq, LANES), lambda bi, i: (bi, 0, i, 0))],
        out_shape=[jax.ShapeDtypeStruct((b, t, W_GRP), F32),
                   jax.ShapeDtypeStruct((b, 2, t, LANES), F32)],
        compiler_params=_params(("parallel", "parallel")),
        name="nsa_cmp_prompt",
    )(proj3, kcmp, vcmp)


def _nsa_att_kernel(q_ref, ks_ref, vs_ref, kw_ref, vw_ref, sel_ref, br_ref, oc_ref, o_ref,
                    ksv, vsv, kwv, vwv, *, tq, t):
    qt = pl.program_id(1)

    @pl.when(qt == 0)
    def _():
        low = _iota((t, LANES), 1) < LANES // 2
        for src, dst in ((ks_ref, ksv), (vs_ref, vsv), (kw_ref, kwv), (vw_ref, vwv)):
            x = src[0]
            xs = pltpu.roll(x, LANES // 2, 1)
            for kvh in range(2):
                for par in range(2):
                    dst[kvh * 2 + par] = _half_variant(x, xs, kvh, par, low)

    q = q_ref[0] * SCALE
    sig = _sigmoid(br_ref[0])
    oc = oc_ref[0]
    rel = (_iota((tq, tq), 0) - _iota((tq, tq), 1)).astype(F32)
    lane = _iota((tq, LANES), 1)
    blk_row = _iota((LANES, tq), 0)
    blk_col = _iota((LANES, tq), 1)
    init1 = (jnp.full((tq, 1), NEG_BIG, F32), jnp.zeros((tq, 1), F32), jnp.zeros((tq, LANES), F32))
    n_win = jnp.minimum(qt, (WINDOW + tq - 1) // tq) + 1

    for kvh in range(2):
        heads = [kvh * 4 + g for g in range(4)]
        qbs = [q[:, (h // 2) * LANES:(h // 2 + 1) * LANES].astype(BF16) for h in heads]
        selk = sel_ref[0, kvh].astype(BF16)

        def tile(kt, state, kv, vv, selected, kvh=kvh, heads=heads, qbs=qbs, selk=selk):
            start = pl.multiple_of(kt * tq, tq)
            dist = rel + ((qt - kt) * tq).astype(F32)
            if selected:
                expand = jnp.where(blk_row == ((start + blk_col) >> 6), 1.0, 0.0).astype(BF16)
                mask = (_dot(selk, expand) > 0.5) & (dist >= 0)
            else:
                mask = (dist >= 0) & (dist < WINDOW)
            bias = jnp.where(mask, 0.0, NEG_BIG)
            kvs = [(kv[kvh * 2 + par, pl.ds(start, tq), :], vv[kvh * 2 + par, pl.ds(start, tq), :])
                   for par in range(2)]
            new = []
            for h, qb, (m, l, acc) in zip(heads, qbs, state):
                k, v = kvs[h % 2]
                s = (_dot_nt(qb, k) - _head_slope(h) * dist) + bias
                m_new = jnp.maximum(m, jnp.max(s, axis=1, keepdims=True))
                p = jnp.exp(s - m_new)
                alpha = jnp.exp(m - m_new)
                new.append((m_new, alpha * l + jnp.sum(p, axis=1, keepdims=True),
                            alpha * acc + _dot(p.astype(BF16), v)))
            return tuple(new)

        init = (init1,) * len(heads)
        st_s = lax.fori_loop(0, qt + 1, lambda i, st: tile(qt - i, st, ksv, vsv, True), init)
        st_w = lax.fori_loop(0, n_win, lambda i, st: tile(qt - i, st, kwv, vwv, False), init)
        o_heads = []
        for i, h in enumerate(heads):
            g_c = sig[:, 3 * h:3 * h + 1]
            g_s = sig[:, 3 * h + 1:3 * h + 2]
            g_w = sig[:, 3 * h + 2:3 * h + 3]
            oc_pair = oc[:, (h // 2) * LANES:(h // 2 + 1) * LANES]
            o_heads.append(g_c * oc_pair + g_s * (st_s[i][2] / st_s[i][1])
                           + g_w * (st_w[i][2] / st_w[i][1]))
        for j in range(2):
            pair = kvh * 2 + j
            o_ref[0, :, pair * LANES:(pair + 1) * LANES] = jnp.where(
                lane < HEAD_DIM, o_heads[2 * j], o_heads[2 * j + 1])


def _nsa_att_prompt(proj3, sel, oc):
    b, t, _ = proj3.shape
    tq = min(t, 256)
    qb = COL["b_q"] // W_GRP
    kv_blocks = [COL[k] // LANES for k in ("b_ks", "b_vs", "b_kw", "b_vw")]
    br = COL["b_br"] // LANES
    in_specs = [pl.BlockSpec((1, tq, W_GRP), lambda bi, i: (bi, i, qb))]
    in_specs += [pl.BlockSpec((1, t, LANES), functools.partial(lambda bi, i, cb: (bi, 0, cb), cb=cb))
                 for cb in kv_blocks]
    in_specs += [pl.BlockSpec((1, 2, tq, LANES), lambda bi, i: (bi, 0, i, 0)),
                 pl.BlockSpec((1, tq, LANES), lambda bi, i: (bi, i, br)),
                 pl.BlockSpec((1, tq, W_GRP), lambda bi, i: (bi, i, 0))]
    return pl.pallas_call(
        functools.partial(_nsa_att_kernel, tq=tq, t=t),
        grid=(b, t // tq),
        in_specs=in_specs,
        out_specs=pl.BlockSpec((1, tq, W_GRP), lambda bi, i: (bi, i, 0)),
        out_shape=jax.ShapeDtypeStruct((b, t, W_GRP), F32),
        scratch_shapes=[pltpu.VMEM((4, t, LANES), BF16)] * 4,
        compiler_params=_params(("parallel", "arbitrary"), 48),
        name="nsa_att_prompt",
    )(proj3, proj3, proj3, proj3, proj3, sel, proj3, oc)


def _pool_kernel(*refs, tt, start, has_halo):
    if has_halo:
        u_ref, halo_ref, pre_ref, w_ref, sc_ref, o_ref, e_ref, d_ref = refs
    else:
        u_ref, pre_ref, w_ref, sc_ref, o_ref, e_ref, d_ref = refs
    ti = pl.program_id(1)
    e_ref[POOL_HALO:POOL_HALO + tt, :] = u_ref[0]
    if has_halo:
        @pl.when(ti == 0)
        def _():
            e_ref[0:POOL_HALO, :] = pre_ref[0]

        @pl.when(ti > 0)
        def _():
            e_ref[0:POOL_HALO, :] = halo_ref[0]
    else:
        e_ref[0:POOL_HALO, :] = pre_ref[0]

    ch = min(tt, 64)
    for c in range(tt // ch):
        r0 = c * ch
        pos = start + ti * tt + r0 + _iota((ch, LANES), 0)
        for g, win in enumerate(POOL_WINDOWS):
            lanes = slice(g * LANES, (g + 1) * LANES)
            cur = e_ref[POOL_HALO + r0:POOL_HALO + r0 + ch, lanes]
            total = cur
            for s in range(1, win):
                total = total + e_ref[POOL_HALO + r0 - s:POOL_HALO + r0 - s + ch, lanes]
            cnt = jnp.minimum(win, pos + 1).astype(F32)
            d_ref[r0:r0 + ch, lanes] = total / cnt - cur
    for g in range(len(POOL_WINDOWS)):
        lanes = slice(g * LANES, (g + 1) * LANES)
        y = _dot(d_ref[:, lanes].astype(BF16), w_ref[g])
        o_ref[0, :, lanes] = y * sc_ref[:, lanes]


def _pool(u_arr, cb, prefix, w_pool, pool_scale, layer, start):
    b, t, _ = u_arr.shape
    tt = min(t, 512)
    nt = t // tt
    has_halo = nt > 1
    in_specs = [pl.BlockSpec((1, tt, W_GRP), lambda bi, i: (bi, i, cb))]
    args = [u_arr]
    if has_halo:
        per = tt // POOL_HALO
        in_specs.append(pl.BlockSpec((1, POOL_HALO, W_GRP),
                                     lambda bi, i: (bi, jnp.maximum(i * per - 1, 0), cb)))
        args.append(u_arr)
    in_specs += [pl.BlockSpec((1, POOL_HALO, W_GRP), lambda bi, i: (bi, 0, 0)),
                 pl.BlockSpec((None, 4, LANES, LANES), lambda bi, i: (layer, 0, 0, 0)),
                 pl.BlockSpec((None, 1, W_GRP), lambda bi, i: (layer, 0, 0))]
    args += [prefix, w_pool, pool_scale]
    return pl.pallas_call(
        functools.partial(_pool_kernel, tt=tt, start=start, has_halo=has_halo),
        grid=(b, nt),
        in_specs=in_specs,
        out_specs=pl.BlockSpec((1, tt, W_GRP), lambda bi, i: (bi, i, 0)),
        out_shape=jax.ShapeDtypeStruct((b, t, W_GRP), F32),
        scratch_shapes=[pltpu.VMEM((POOL_HALO + tt, W_GRP), F32), pltpu.VMEM((tt, W_GRP), F32)],
        compiler_params=_params(("parallel", "arbitrary")),
        name="pool",
    )(*args)


def _conv_kernel(*refs, tt, has_halo):
    if has_halo:
        (a_ref, g_ref, ha_ref, hg_ref, pre_ref, cw_ref, cb_ref, lg_ref, lb_ref, pw_ref,
         o_ref, u_ref, e_ref, y_ref) = refs
    else:
        (a_ref, g_ref, pre_ref, cw_ref, cb_ref, lg_ref, lb_ref, pw_ref,
         o_ref, u_ref, e_ref, y_ref) = refs
    ti = pl.program_id(1)
    u = a_ref[0] * _sigmoid(g_ref[0])
    u_ref[0] = u
    e_ref[CONV_HALO:CONV_HALO + tt, :] = u
    if has_halo:
        @pl.when(ti == 0)
        def _():
            e_ref[0:CONV_HALO, :] = pre_ref[0]

        @pl.when(ti > 0)
        def _():
            e_ref[0:CONV_HALO, :] = ha_ref[0] * _sigmoid(hg_ref[0])
    else:
        e_ref[0:CONV_HALO, :] = pre_ref[0]

    ch = min(tt, 32)
    lead = CONV_HALO - (CONV_W - 1)
    for c in range(tt // ch):
        r0 = c * ch
        acc = jnp.broadcast_to(cb_ref[...], (ch, W_GRP))
        for j in range(CONV_W):
            acc = acc + cw_ref[j:j + 1, :] * e_ref[r0 + lead + j:r0 + lead + j + ch, :]
        y_ref[r0:r0 + ch, :] = acc
    y = y_ref[...]
    yc = y - jnp.mean(y, axis=-1, keepdims=True)
    yn = yc * lax.rsqrt(jnp.mean(yc * yc, axis=-1, keepdims=True) + LN_EPS)
    yn = yn * lg_ref[...] + lb_ref[...]
    o_ref[0] = _dot((yn * _sigmoid(yn)).astype(BF16), pw_ref[...])


def _conv(arr, cb_a, cb_g, prefix, conv_w, conv_b, ln_g, ln_b, w_pw, layer):
    b, t, _ = arr.shape
    tt = min(t, 512)
    nt = t // tt
    has_halo = nt > 1
    in_specs = [pl.BlockSpec((1, tt, W_GRP), lambda bi, i: (bi, i, cb_a)),
                pl.BlockSpec((1, tt, W_GRP), lambda bi, i: (bi, i, cb_g))]
    args = [arr, arr]
    if has_halo:
        per = tt // CONV_HALO
        in_specs += [pl.BlockSpec((1, CONV_HALO, W_GRP),
                                  lambda bi, i: (bi, jnp.maximum(i * per - 1, 0), cb_a)),
                     pl.BlockSpec((1, CONV_HALO, W_GRP),
                                  lambda bi, i: (bi, jnp.maximum(i * per - 1, 0), cb_g))]
        args += [arr, arr]
    vec = lambda: pl.BlockSpec((None, 1, W_GRP), lambda bi, i: (layer, 0, 0))
    in_specs += [pl.BlockSpec((1, CONV_HALO, W_GRP), lambda bi, i: (bi, 0, 0)),
                 pl.BlockSpec((None, CONV_HALO, W_GRP), lambda bi, i: (layer, 0, 0)),
                 vec(), vec(), vec(),
                 pl.BlockSpec((None, W_GRP, W_GRP), lambda bi, i: (layer, 0, 0))]
    args += [prefix, conv_w, conv_b, ln_g, ln_b, w_pw]
    return pl.pallas_call(
        functools.partial(_conv_kernel, tt=tt, has_halo=has_halo),
        grid=(b, nt),
        in_specs=in_specs,
        out_specs=[pl.BlockSpec((1, tt, W_GRP), lambda bi, i: (bi, i, 0))] * 2,
        out_shape=[jax.ShapeDtypeStruct((b, t, W_GRP), F32)] * 2,
        scratch_shapes=[pltpu.VMEM((CONV_HALO + tt, W_GRP), F32), pltpu.VMEM((tt, W_GRP), F32)],
        compiler_params=_params(("parallel", "arbitrary")),
        name="conv",
    )(*args)


SB_ROWS = 32
PAGES_PER_STEP = 8


def _sb_sample_kernel(pt_ref, q_ref, kn_ref, vn_ref, k_hbm, v_hbm, o_ref,
                      kbuf, vbuf, sem, carry_ref, acc_ref, *, layer, n_pages, page):
    b = pl.program_id(0)
    qbd = q_ref[0].astype(BF16)
    upper = _upper_ones(page)

    def page_copies(i, slot):
        pg = pt_ref[b * n_pages + (n_pages - 1 - i)]
        return (pltpu.make_async_copy(k_hbm.at[layer, pg], kbuf.at[slot], sem.at[0, slot]),
                pltpu.make_async_copy(v_hbm.at[layer, pg], vbuf.at[slot], sem.at[1, slot]))

    def start(i, slot):
        for c in page_copies(i, slot):
            c.start()

    def wait(i, slot):
        for c in page_copies(i, slot):
            c.wait()

    start(0, 0)

    def absorb(k, v, mask, channel_major):
        kb, vb = k.astype(BF16), v.astype(BF16)
        z = _dot(qbd, kb) if channel_major else _dot_nt(qbd, kb)
        lk = _log_keep(z)
        if mask is not None:
            lk = jnp.where(mask, lk, 0.0)
        between = _split_dot(lk, upper) + carry_ref[...]
        a = jnp.exp(z + lk + between)
        if mask is not None:
            a = jnp.where(mask, a, 0.0)
        ab = a.astype(BF16)
        acc_ref[...] += _dot_nt(ab, vb) if channel_major else _dot(ab, vb)
        carry_ref[...] += jnp.sum(lk, axis=1, keepdims=True)

    carry_ref[...] = jnp.zeros_like(carry_ref)
    acc_ref[...] = jnp.zeros_like(acc_ref)
    pad = jnp.zeros((page - kn_ref.shape[1], W_GRP), F32)
    r = _iota((SB_ROWS, page), 0)
    j = _iota((SB_ROWS, page), 1)
    absorb(jnp.concatenate([kn_ref[0], pad], axis=0),
           jnp.concatenate([vn_ref[0], pad], axis=0), j < (r >> 3), False)

    def live():
        return jnp.max(carry_ref[...]) > SB_DEAD_CARRY

    def body(c):
        i = c[0]
        slot = i & 1
        wait(i, slot)

        @pl.when(i + 1 < n_pages)
        def _():
            start(i + 1, 1 - slot)

        absorb(kbuf[slot].reshape(W_GRP, page), vbuf[slot].reshape(W_GRP, page), None, True)
        return i + 1, live()

    i_end, _ = lax.while_loop(lambda c: (c[0] < n_pages) & c[1], body, (jnp.int32(0), live()))

    @pl.when(i_end < n_pages)
    def _():
        wait(i_end, i_end & 1)

    acc = acc_ref[...]
    r = _iota((SB_ROWS, HEAD_DIM), 0)
    out = jnp.zeros((SB_ROWS, HEAD_DIM), F32)
    for h in range(W_GRP // HEAD_DIM):
        out = jnp.where((r & 7) == h, acc[:, h * HEAD_DIM:(h + 1) * HEAD_DIM], out)
    o_ref[0] = out


def _sb_sample(page_table, qbd, k_new, v_new, cache_k, cache_v, layer):
    db, n_pages = page_table.shape
    n_heads, page = cache_k.shape[2], cache_k.shape[4]
    small = lambda r: pl.BlockSpec((1, r, W_GRP), lambda bi, pt: (bi, 0, 0))
    grid_spec = pltpu.PrefetchScalarGridSpec(
        num_scalar_prefetch=1,
        grid=(db,),
        in_specs=[small(SB_ROWS), small(k_new.shape[1]), small(v_new.shape[1]),
                  pl.BlockSpec(memory_space=pl.ANY), pl.BlockSpec(memory_space=pl.ANY)],
        out_specs=pl.BlockSpec((1, SB_ROWS, HEAD_DIM), lambda bi, pt: (bi, 0, 0)),
        scratch_shapes=[pltpu.VMEM((2, n_heads, HEAD_DIM, page), F32),
                        pltpu.VMEM((2, n_heads, HEAD_DIM, page), F32),
                        pltpu.SemaphoreType.DMA((2, 2)),
                        pltpu.VMEM((SB_ROWS, 1), F32), pltpu.VMEM((SB_ROWS, W_GRP), F32)],
    )
    return pl.pallas_call(
        functools.partial(_sb_sample_kernel, layer=layer, n_pages=n_pages, page=page),
        grid_spec=grid_spec,
        out_shape=jax.ShapeDtypeStruct((db, SB_ROWS, HEAD_DIM), F32),
        compiler_params=_params(("arbitrary",)),
        name="sb_sample",
    )(page_table.reshape(-1), qbd, k_new, v_new, cache_k, cache_v)


def _row_fields(shape):
    r = _iota(shape, 0)
    qi = (r >> 1) & 3
    kvh = r & 1
    h = kvh * 4 + (r >> 3)
    slope = jnp.zeros(shape, F32)
    for hh in range(8):
        slope = jnp.where(h == hh, _head_slope(hh), slope)
    return qi, kvh, slope


def _pick_half(x, kvh64):
    return jnp.where(kvh64 == 0, x[:, :HEAD_DIM], x[:, HEAD_DIM:])


def _nsa_small_kernel(pt_ref, q_ref, kall_ref, vall_ref, kwc_ref, vwc_ref, kwn_ref, vwn_ref,
                      oc_ref, ow_ref, sel_ref, used_ref, kg_ref, vg_ref, *, n_pages, past):
    b = pl.program_id(0)

    def gather(p, carry):
        idx = pt_ref[b * n_pages + p]
        kg_ref[pl.ds(p, 1), :] = kall_ref[pl.ds(idx, 1), :]
        vg_ref[pl.ds(p, 1), :] = vall_ref[pl.ds(idx, 1), :]
        return carry

    lax.fori_loop(0, n_pages, gather, 0)

    q = q_ref[0].astype(BF16)
    rows = q_ref.shape[1]
    _, kvh64, _ = _row_fields((rows, HEAD_DIM))

    qi, _, slope = _row_fields((rows, n_pages))
    pidx = _iota((rows, n_pages), 1)
    qpos = past + qi
    per_page = kg_ref.shape[1] // LANES
    scores, oks = [], []
    for n4 in range(per_page):
        z = _dot_nt(q, kg_ref[:, n4 * LANES:(n4 + 1) * LANES].astype(BF16))
        dist = (qpos - ((pidx * per_page + n4) * L_CMP + (L_CMP - 1))).astype(F32)
        ok = dist >= 0
        oks.append(ok)
        scores.append(jnp.where(ok, z - slope * dist, NEG_BIG))
    m = functools.reduce(jnp.maximum, [jnp.max(s, axis=1, keepdims=True) for s in scores])
    es = [jnp.exp(s - m) for s in scores]
    tot = functools.reduce(jnp.add, [jnp.sum(e, axis=1, keepdims=True) for e in es])
    ps = [jnp.where(ok, e / tot, 0.0) for ok, e in zip(oks, es)]
    o_c = functools.reduce(jnp.add, [
        _dot(p.astype(BF16), vg_ref[:, n4 * LANES:(n4 + 1) * LANES].astype(BF16))
        for n4, p in enumerate(ps)])
    oc_ref[0] = _pick_half(o_c, kvh64)

    def group_sum(x):
        return x[0:8] + x[8:16] + x[16:24] + x[24:32]

    lane = _iota((8, n_pages), 1)
    s_even = group_sum(ps[0] + ps[1]) + jnp.where(lane == 0, FORCE_BONUS, 0.0)
    s_odd = group_sum(ps[2] + ps[3]) + jnp.where(lane == n_pages - 1, FORCE_BONUS, 0.0)
    score = jnp.concatenate([s_even, s_odd], axis=1)
    lane2 = _iota((8, 2 * n_pages), 1)
    blk = jnp.where(lane2 < n_pages, 2 * lane2, 2 * (lane2 - n_pages) + 1)
    sel = jnp.zeros((8, 2 * n_pages), F32)
    for _ in range(N_SEL - 1):
        best = jnp.max(score, axis=1, keepdims=True)
        first = jnp.min(jnp.where(score == best, blk, 2 ** 30), axis=1, keepdims=True)
        hit = (blk == first) & (best > -jnp.inf)
        sel = jnp.where(hit, 1.0, sel)
        score = jnp.where(blk == first, -jnp.inf, score)
    sel_ref[0] = sel
    any_row = jnp.max(sel, axis=0, keepdims=True)
    used = jnp.maximum(any_row[:, :n_pages], any_row[:, n_pages:])
    used_ref[0] = jnp.broadcast_to(used, (8, n_pages)).astype(jnp.int32)

    wbuf = kwc_ref.shape[4]
    n_new = kwn_ref.shape[1]
    padw = jnp.zeros((LANES - n_new, LANES), F32)
    kw_new = jnp.concatenate([kwn_ref[0], padw], axis=0).astype(BF16)
    vw_new = jnp.concatenate([vwn_ref[0], padw], axis=0).astype(BF16)
    kw_old = kwc_ref[0, 0].reshape(LANES, wbuf).astype(BF16)
    vw_old = vwc_ref[0, 0].reshape(LANES, wbuf).astype(BF16)

    def scores(z, dist):
        ok = (dist >= 0) & (dist < WINDOW)
        return jnp.where(ok, z - slope_of(dist.shape) * dist.astype(F32), NEG_BIG), ok

    def slope_of(shape):
        return _row_fields(shape)[2]

    qi_o, _, _ = _row_fields((rows, wbuf))
    qi_n, _, _ = _row_fields((rows, LANES))
    s_o, ok_o = scores(_dot(q, kw_old), wbuf + qi_o - _iota((rows, wbuf), 1))
    s_n, ok_n = scores(_dot_nt(q, kw_new), qi_n - _iota((rows, LANES), 1))
    m = jnp.maximum(jnp.max(s_o, axis=1, keepdims=True), jnp.max(s_n, axis=1, keepdims=True))
    e_o, e_n = jnp.exp(s_o - m), jnp.exp(s_n - m)
    tot = jnp.sum(e_o, axis=1, keepdims=True) + jnp.sum(e_n, axis=1, keepdims=True)
    p_o = jnp.where(ok_o, e_o / tot, 0.0).astype(BF16)
    p_n = jnp.where(ok_n, e_n / tot, 0.0).astype(BF16)
    ow_ref[0] = _pick_half(_dot_nt(p_o, vw_old) + _dot(p_n, vw_new), kvh64)


def _nsa_small(page_table, q_rows, kall, vall, win_k, win_v, kw_new, vw_new, layer, past):
    db, n_pages = page_table.shape
    rows = q_rows.shape[1]
    wbuf = win_k.shape[4]
    whole = lambda a: pl.BlockSpec(a.shape, lambda bi, pt: (0,) * a.ndim)
    per_b = lambda a: pl.BlockSpec((1,) + a.shape[1:], lambda bi, pt: (bi,) + (0,) * (a.ndim - 1))
    win = pl.BlockSpec((1, 1, 2, HEAD_DIM, wbuf), lambda bi, pt: (layer, bi, 0, 0, 0))
    grid_spec = pltpu.PrefetchScalarGridSpec(
        num_scalar_prefetch=1,
        grid=(db,),
        in_specs=[per_b(q_rows), whole(kall), whole(vall), win, win, per_b(kw_new), per_b(vw_new)],
        out_specs=[pl.BlockSpec((1, rows, HEAD_DIM), lambda bi, pt: (bi, 0, 0)),
                   pl.BlockSpec((1, rows, HEAD_DIM), lambda bi, pt: (bi, 0, 0)),
                   pl.BlockSpec((1, 8, 2 * n_pages), lambda bi, pt: (bi, 0, 0)),
                   pl.BlockSpec((1, 8, n_pages), lambda bi, pt: (bi, 0, 0))],
        scratch_shapes=[pltpu.VMEM((n_pages, kall.shape[1]), F32)] * 2,
    )
    return pl.pallas_call(
        functools.partial(_nsa_small_kernel, n_pages=n_pages, past=past),
        grid_spec=grid_spec,
        out_shape=[jax.ShapeDtypeStruct((db, rows, HEAD_DIM), F32),
                   jax.ShapeDtypeStruct((db, rows, HEAD_DIM), F32),
                   jax.ShapeDtypeStruct((db, 8, 2 * n_pages), F32),
                   jax.ShapeDtypeStruct((db, 8, n_pages), jnp.int32)],
        compiler_params=_params(("arbitrary",)),
        name="nsa_small_sample",
    )(page_table.reshape(-1), q_rows, kall, vall, win_k, win_v, kw_new, vw_new)


def _nsa_sel_kernel(pt_ref, used_ref, q_ref, sel_ref, kn_ref, vn_ref, oc_ref, ow_ref, br_ref, *rest,
                    npg, n_pages, page, past):
    del pt_ref
    krefs, vrefs = rest[:npg], rest[npg:2 * npg]
    o_ref, m_ref, l_ref, acc_ref = rest[2 * npg:]
    b = pl.program_id(0)
    step = pl.program_id(1)
    q = q_ref[0].astype(BF16)
    rows = q_ref.shape[1]
    qi, _, slope = _row_fields((rows, page))
    lane = _iota((rows, page), 1)
    qpos = past + qi

    def absorb(k, v, kpos0, mask, channel_major):
        kb, vb = k.astype(BF16), v.astype(BF16)
        dist = (qpos - (kpos0 + lane)).astype(F32)
        s = (_dot(q, kb) if channel_major else _dot_nt(q, kb)) - slope * dist
        if mask is None:
            mask = dist >= 0
        m_old = m_ref[...]
        m_new = jnp.maximum(m_old, jnp.max(jnp.where(mask, s, NEG_BIG), axis=1, keepdims=True))
        p = jnp.where(mask, jnp.exp(s - m_new), 0.0)
        alpha = jnp.exp(m_old - m_new)
        l_ref[...] = alpha * l_ref[...] + jnp.sum(p, axis=1, keepdims=True)
        pb = p.astype(BF16)
        acc_ref[...] = alpha * acc_ref[...] + (_dot_nt(pb, vb) if channel_major else _dot(pb, vb))
        m_ref[...] = m_new

    @pl.when(step == 0)
    def _():
        m_ref[...] = jnp.full_like(m_ref, NEG_BIG)
        l_ref[...] = jnp.zeros_like(l_ref)
        acc_ref[...] = jnp.zeros_like(acc_ref)
        pad = jnp.zeros((page - kn_ref.shape[1], LANES), F32)
        absorb(jnp.concatenate([kn_ref[0], pad], axis=0),
               jnp.concatenate([vn_ref[0], pad], axis=0), past, None, False)

    selm = jnp.concatenate([sel_ref[0]] * (rows // 8), axis=0)
    pl_lane = _iota((rows, n_pages), 1)
    per_blk = page // L_SEL
    for g in range(npg):
        pg = n_pages - 1 - (step * npg + g)

        @pl.when(used_ref[b * n_pages + pg] > 0)
        def _(g=g, pg=pg):
            mask = jnp.zeros((rows, page), jnp.bool_)
            for j in range(per_blk):
                cj = jnp.sum(jnp.where(pl_lane == pg, selm[:, j * n_pages:(j + 1) * n_pages], 0.0),
                             axis=1, keepdims=True)
                mask = mask | ((cj > 0.5) & (lane >= j * L_SEL) & (lane < (j + 1) * L_SEL))
            absorb(krefs[g][0, 0].reshape(LANES, page), vrefs[g][0, 0].reshape(LANES, page),
                   pg * page, mask, True)

    @pl.when(step == pl.num_programs(1) - 1)
    def _():
        _, kvh64, _ = _row_fields((rows, HEAD_DIM))
        o_s = _pick_half(acc_ref[...] / l_ref[...], kvh64)
        sig = _sigmoid(br_ref[0])
        o_ref[0] = sig[:, 0:1] * oc_ref[0] + sig[:, 1:2] * o_s + sig[:, 2:3] * ow_ref[0]


def _nsa_sel_sample(page_table, used, q_rows, sel, k_new, v_new, oc, ow, br_rows, cache_k, cache_v,
                    layer, past):
    db, n_pages = page_table.shape
    rows = q_rows.shape[1]
    page = cache_k.shape[4]
    npg = min(PAGES_PER_STEP, n_pages)
    steps = n_pages // npg

    def page_spec(g):
        def index(bi, s, pt, us):
            return (layer, pt[bi * n_pages + (n_pages - 1 - (s * npg + g))], 0, 0, 0)
        return pl.BlockSpec((1, 1, 2, HEAD_DIM, page), index)

    per_b = lambda a: pl.BlockSpec((1,) + a.shape[1:],
                                   lambda bi, s, pt, us: (bi,) + (0,) * (a.ndim - 1))
    grid_spec = pltpu.PrefetchScalarGridSpec(
        num_scalar_prefetch=2,
        grid=(db, steps),
        in_specs=[per_b(a) for a in (q_rows, sel, k_new, v_new, oc, ow, br_rows)]
        + [page_spec(g) for g in range(npg)] * 2,
        out_specs=pl.BlockSpec((1, rows, HEAD_DIM), lambda bi, s, pt, us: (bi, 0, 0)),
        scratch_shapes=[pltpu.VMEM((rows, 1), F32), pltpu.VMEM((rows, 1), F32),
                        pltpu.VMEM((rows, LANES), F32)],
    )
    return pl.pallas_call(
        functools.partial(_nsa_sel_kernel, npg=npg, n_pages=n_pages, page=page, past=past),
        grid_spec=grid_spec,
        out_shape=jax.ShapeDtypeStruct((db, rows, HEAD_DIM), F32),
        compiler_params=_params(("parallel", "arbitrary")),
        name="nsa_sel_sample",
    )(page_table.reshape(-1), used.reshape(-1), q_rows, sel, k_new, v_new, oc, ow, br_rows,
      *([cache_k] * npg), *([cache_v] * npg))


def _pack_w_in(w_in):
    depth, d, _ = w_in.shape
    wt = jnp.transpose(w_in, (0, 2, 1))
    pieces, src = [], 0
    placed = {}
    for name, width in _SRC:
        placed[name] = (src, width)
        src += width
    order = sorted(COL, key=COL.get)
    pos = 0
    for name in order:
        assert COL[name] == pos
        s, width = placed[name]
        pieces.append(wt[:, s:s + width, :])
        pos += width
    pieces.append(jnp.zeros((depth, N_PACK - pos, d), w_in.dtype))
    return jnp.concatenate(pieces, axis=1).astype(BF16)


def _expand_cmp_weight(w):
    depth, l, d, e = w.shape
    eye = jnp.eye(2, dtype=w.dtype)
    return jnp.einsum("zlde,hg->zlhdge", w, eye).reshape(depth, l * 2 * d, 2 * e).astype(BF16)


def _expand_cmp_weight_t(w, page):
    depth, l, d, e = w.shape
    n = page // l
    eye = jnp.eye(n, dtype=w.dtype)
    return jnp.einsum("zlde,nm->zdnlme", w, eye).reshape(depth, d, page, n * e).astype(BF16)


def _sample_rows(x, n_heads_last):
    db, tq = x.shape[:2]
    x = x.reshape(db, tq, 2, 4, n_heads_last)
    return x.transpose(0, 3, 1, 2, 4).reshape(db, 4 * tq * 2, n_heads_last)


def kernel(x_prompt, x_sample, cache_sb_k, cache_sb_v, cache_cmp_k, cache_cmp_v, cache_sel_k, cache_sel_v, cache_win_k, cache_win_v, state_pool, state_conv, page_table, norm_g, w_in, w_cmp_k, w_cmp_v, w_pool, pool_scale, conv_w, conv_b, ln_g, ln_b, w_pw, w_out, final_g):
    bp, seq, d_model = x_prompt.shape
    db, tq_s, _ = x_sample.shape
    depth, n_pool, page = cache_sb_k.shape[:3]
    assert tq_s == 4 and page == LANES and cache_win_k.shape[2] == WINDOW and seq % 256 == 0

    w = _prepare_weights(norm_g, w_in, w_cmp_k, w_cmp_v, w_pool, pool_scale, conv_w, conv_b,
                         ln_g, ln_b, w_pw, w_out, final_g)
    chan_major = lambda c: jnp.transpose(c, (0, 1, 3, 4, 2))
    sbk, sbv = chan_major(cache_sb_k), chan_major(cache_sb_v)
    selk, selv = chan_major(cache_sel_k), chan_major(cache_sel_v)
    cmpk = chan_major(cache_cmp_k).reshape(depth * n_pool * 2 * HEAD_DIM, page)
    cmpv = chan_major(cache_cmp_v).reshape(depth * n_pool * 2 * HEAD_DIM, page)

    hp = x_prompt.reshape(bp * seq, d_model)
    hs = x_sample.reshape(db * tq_s, d_model)
    p_states, s_states = [], []
    for l in range(depth):
        last = l == depth - 1
        hp, st = _prompt_layer(hp, bp, seq, w, l, last)
        p_states.append(st)
        hs, st = _sample_layer(hs, db, tq_s, w, l, last, page_table, sbk, sbv, cmpk, cmpv, selk, selv,
                               cache_win_k, cache_win_v, state_pool, state_conv)
        s_states.append(st)

    stacked = lambda states, j: jnp.stack([st[j] for st in states], axis=0)
    return (hp.reshape(bp, seq, d_model), hs.reshape(db, tq_s, d_model),
            *[stacked(p_states, j) for j in range(10)],
            *[stacked(s_states, j) for j in range(10)])


def _prepare_weights(norm_g, w_in, w_cmp_k, w_cmp_v, w_pool, pool_scale, conv_w, conv_b, ln_g, ln_b,
                     w_pw, w_out, final_g):
    depth = w_in.shape[0]
    vec3 = lambda a: a.reshape(depth, 1, -1)
    return dict(
        w_pack=_pack_w_in(w_in), w_out=w_out.astype(BF16),
        wk_exp=_expand_cmp_weight(w_cmp_k), wv_exp=_expand_cmp_weight(w_cmp_v),
        wk_t=_expand_cmp_weight_t(w_cmp_k, LANES), wv_t=_expand_cmp_weight_t(w_cmp_v, LANES),
        w_pool=w_pool.astype(BF16), w_pw=w_pw.astype(BF16),
        conv_w=jnp.pad(conv_w, ((0, 0), (0, CONV_HALO - CONV_W), (0, 0))),
        norm_g=vec3(norm_g), pool_scale=vec3(pool_scale), conv_b=vec3(conv_b),
        ln_g=vec3(ln_g), ln_b=vec3(ln_b), final_g=final_g.reshape(1, -1))


def _prompt_layer(hp, bp, seq, w, l, last):
    proj = _inproj(hp, w["norm_g"], w["w_pack"], l)
    proj3 = proj.reshape(bp, seq, N_PACK)
    col = lambda name, width: proj3[:, :, COL[name]:COL[name] + width]
    kc, vc = col("b_kc", LANES), col("b_vc", LANES)
    o_a = _sb_prompt(proj3)
    n_blocks = bp * seq // L_CMP
    kcmp, vcmp = _compress(kc.reshape(n_blocks, L_CMP * LANES), vc.reshape(n_blocks, L_CMP * LANES),
                           w["wk_exp"], w["wv_exp"], l, 0, n_blocks)
    n_cmp = seq // L_CMP
    o_cmp, sel = _nsa_cmp_prompt(proj3, kcmp.reshape(bp, n_cmp, LANES), vcmp.reshape(bp, n_cmp, LANES))
    o_b = _nsa_att_prompt(proj3, sel, o_cmp)
    zero_pool = jnp.zeros((bp, POOL_HALO, W_GRP), F32)
    zero_conv = jnp.zeros((bp, CONV_HALO, W_GRP), F32)
    o_c = _pool(proj3, COL["c_u"] // W_GRP, zero_pool, w["w_pool"], w["pool_scale"], l, 0)
    o_d, u_conv = _conv(proj3, COL["d_a"] // W_GRP, COL["d_gl"] // W_GRP, zero_conv,
                        w["conv_w"], w["conv_b"], w["ln_g"], w["ln_b"], w["w_pw"], l)
    flat = lambda a: a.reshape(bp * seq, W_GRP)
    hp_new = _outproj([flat(o_a), flat(o_b), flat(o_c), flat(o_d)], proj, hp, w["w_out"], l,
                      w["final_g"] if last else None)
    keep = min(WINDOW, seq)
    heads = lambda a, n: a.reshape(bp, -1, n, HEAD_DIM)
    states = (
        heads(col("a_k", W_GRP), 8), heads(col("a_v", W_GRP), 8),
        heads(kc, 2), heads(vc, 2),
        heads(col("b_ks", LANES), 2), heads(col("b_vs", LANES), 2),
        heads(col("b_kw", LANES)[:, seq - keep:], 2), heads(col("b_vw", LANES)[:, seq - keep:], 2),
        col("c_u", W_GRP)[:, seq - (POOL_HALO - 1):], u_conv[:, seq - (CONV_W - 1):])
    return hp_new, states


def _sample_layer(hs, db, tq_s, w, l, last, page_table, sbk, sbv, cmpk, cmpv, selk, selv,
                  cache_win_k, cache_win_v, state_pool, state_conv):
    depth, n_pool = sbk.shape[:2]
    page = sbk.shape[4]
    n_pages = page_table.shape[1]
    past = n_pages * page
    projs = _inproj(hs, w["norm_g"], w["w_pack"], l)
    projs3 = projs.reshape(db, tq_s, N_PACK)
    scol = lambda name, width: projs3[:, :, COL[name]:COL[name] + width]
    pad8 = lambda a: jnp.pad(a, ((0, 0), (0, 8 - tq_s), (0, 0)))
    head_of_col = jnp.arange(W_GRP) // HEAD_DIM
    sb_mask = (head_of_col[None, :] == jnp.arange(8)[:, None]).astype(F32)
    a_q = scol("a_q", W_GRP) * SCALE
    qbd = (a_q[:, :, None, :] * sb_mask[None, None]).reshape(db, tq_s * 8, W_GRP)
    o_a = _sb_sample(page_table, qbd, pad8(scol("a_k", W_GRP)), pad8(scol("a_v", W_GRP)), sbk, sbv, l)
    o_a = o_a.reshape(db * tq_s, W_GRP)
    kall = _compress_cache(cmpk, w["wk_t"], l, n_pool * 2)
    vall = _compress_cache(cmpv, w["wv_t"], l, n_pool * 2)
    per_page = page // L_CMP
    relane = lambda a: a.reshape(n_pool, 2, per_page, HEAD_DIM).transpose(0, 2, 1, 3).reshape(
        n_pool, per_page * LANES)
    kall, vall = relane(kall), relane(vall)
    b_q = _sample_rows(scol("b_q", W_GRP) * SCALE, HEAD_DIM)
    kvh_of_row = jnp.arange(b_q.shape[1]) & 1
    q_rows = (b_q[:, :, None, :] * jnp.eye(2, dtype=F32)[kvh_of_row][None, :, :, None]).reshape(db, -1, LANES)
    br_rows = jnp.pad(_sample_rows(scol("b_br", 24), 3), ((0, 0), (0, 0), (0, LANES - 3)))
    kw_new, vw_new = scol("b_kw", LANES), scol("b_vw", LANES)
    wink = jnp.transpose(cache_win_k, (0, 1, 3, 4, 2))
    winv = jnp.transpose(cache_win_v, (0, 1, 3, 4, 2))
    o_cmp, o_win, sel, used = _nsa_small(page_table, q_rows, kall, vall, wink, winv,
                                         pad8(kw_new), pad8(vw_new), l, past)
    o_b_rows = _nsa_sel_sample(page_table, used[:, 0], q_rows, sel, pad8(scol("b_ks", LANES)),
                               pad8(scol("b_vs", LANES)), o_cmp, o_win, br_rows, selk, selv, l, past)
    o_b = o_b_rows.reshape(db, 4, tq_s, 2, HEAD_DIM).transpose(0, 2, 3, 1, 4).reshape(db * tq_s, W_GRP)
    pool_pre = jnp.pad(state_pool[l], ((0, 0), (POOL_HALO - state_pool.shape[2], 0), (0, 0)))
    o_c = _pool(pad8(scol("c_u", W_GRP)), 0, pool_pre, w["w_pool"], w["pool_scale"], l, past)
    o_c = o_c[:, :tq_s].reshape(db * tq_s, W_GRP)
    conv_pre = jnp.pad(state_conv[l], ((0, 0), (CONV_HALO - state_conv.shape[2], 0), (0, 0)))
    o_d, u_conv = _conv(pad8(scol("d_a", 2 * W_GRP)), 0, 1, conv_pre,
                        w["conv_w"], w["conv_b"], w["ln_g"], w["ln_b"], w["w_pw"], l)
    o_d = o_d[:, :tq_s].reshape(db * tq_s, W_GRP)
    hs_new = _outproj([o_a, o_b, o_c, o_d], projs, hs, w["w_out"], l, w["final_g"] if last else None)
    sheads = lambda a, n: a.reshape(db, -1, n, HEAD_DIM)
    states = (
        sheads(scol("a_k", W_GRP), 8), sheads(scol("a_v", W_GRP), 8),
        sheads(scol("b_kc", LANES), 2), sheads(scol("b_vc", LANES), 2),
        sheads(scol("b_ks", LANES), 2), sheads(scol("b_vs", LANES), 2),
        jnp.concatenate([cache_win_k[l][:, tq_s:], sheads(kw_new, 2)], axis=1),
        jnp.concatenate([cache_win_v[l][:, tq_s:], sheads(vw_new, 2)], axis=1),
        jnp.concatenate([state_pool[l][:, tq_s:], scol("c_u", W_GRP)], axis=1),
        jnp.concatenate([state_conv[l][:, tq_s:], u_conv[:, :tq_s]], axis=1))
    return hs_new, states
```

```python
import functools

import jax
import jax.numpy as jnp
from jax import lax
from jax.experimental import pallas as pl
from jax.experimental.pallas import tpu as pltpu

F32 = jnp.float32
BF16 = jnp.bfloat16

HEAD_DIM = 64
LANES = 128
SUBLANES = 8
W_GRP = 512
L_CMP = 32
L_SEL = 64
N_SEL = 16
WINDOW = 512
FORCE_BONUS = 1000.0
POOL_WINDOWS = (2, 4, 8, 16)
POOL_HALO = 16
CONV_W = 31
CONV_HALO = 32
RMS_EPS = 1e-6
LN_EPS = 1e-5
NEG_BIG = -1e30
SCALE = HEAD_DIM ** -0.5
SB_DEAD_CARRY = -104.0

COL = dict(a_q=0, a_k=512, a_v=1024, a_g=1536, b_q=2048, b_g=2560, c_u=3072, c_g=3584,
           d_a=4096, d_gl=4608, d_g=5120, b_kc=5632, b_vc=5760, b_ks=5888, b_vs=6016,
           b_kw=6144, b_vw=6272, b_br=6400)
N_PACK = 6656
_SRC = (("a_q", 512), ("a_k", 512), ("a_v", 512), ("a_g", 512), ("b_q", 512), ("b_kc", 128),
        ("b_vc", 128), ("b_ks", 128), ("b_vs", 128), ("b_kw", 128), ("b_vw", 128), ("b_g", 512),
        ("b_br", 24), ("c_u", 512), ("c_g", 512), ("d_a", 512), ("d_gl", 512), ("d_g", 512))


def _params(sem, vmem_mb=None):
    kw = dict(dimension_semantics=sem)
    if vmem_mb is not None:
        kw["vmem_limit_bytes"] = vmem_mb * 2 ** 20
    return pltpu.CompilerParams(**kw)


def _sigmoid(x):
    return 1.0 / (1.0 + jnp.exp(-x))


def _iota(shape, axis):
    return lax.broadcasted_iota(jnp.int32, shape, axis)


def _dot_nt(a, b):
    return lax.dot_general(a, b, (((1,), (1,)), ((), ())), preferred_element_type=F32)


def _dot(a, b):
    return jnp.dot(a, b, preferred_element_type=F32)


def _split_dot(x, m):
    hi = x.astype(BF16)
    lo = (x - hi.astype(F32)).astype(BF16)
    return _dot(hi, m) + _dot(lo, m)


def _log_keep(z):
    return jnp.minimum(-z, 0.0) - jnp.log(1.0 + jnp.exp(-jnp.abs(z)))


def _upper_ones(n):
    return jnp.where(_iota((n, n), 0) > _iota((n, n), 1), 1.0, 0.0).astype(BF16)


def _inproj_kernel(x_ref, g_ref, w_ref, o_ref, hn_ref):
    @pl.when(pl.program_id(1) == 0)
    def _():
        x = x_ref[...]
        ms = jnp.mean(x * x, axis=-1, keepdims=True)
        hn_ref[...] = ((x * lax.rsqrt(ms + RMS_EPS)) * g_ref[...]).astype(BF16)

    o_ref[...] = _dot_nt(hn_ref[...], w_ref[...])


def _inproj(x, norm_g, w_pack, layer):
    rows, d = x.shape
    n = w_pack.shape[1]
    tm = min(rows, 1024)
    tn = n // 4 if (n // 4) % LANES == 0 else 512
    return pl.pallas_call(
        _inproj_kernel,
        grid=(rows // tm, n // tn),
        in_specs=[pl.BlockSpec((tm, d), lambda i, j: (i, 0)),
                  pl.BlockSpec((None, 1, d), lambda i, j: (layer, 0, 0)),
                  pl.BlockSpec((None, tn, d), lambda i, j: (layer, j, 0))],
        out_specs=pl.BlockSpec((tm, tn), lambda i, j: (i, j)),
        out_shape=jax.ShapeDtypeStruct((rows, n), F32),
        scratch_shapes=[pltpu.VMEM((tm, d), BF16)],
        compiler_params=_params(("parallel", "arbitrary"), 56),
        name="inproj",
    )(x, norm_g, w_pack)


def _outproj_kernel(oa, ob, oc, od, ga, gb, gc, gd, h_ref, w_ref, *rest, final):
    if final:
        fg_ref, o_ref = rest
    else:
        (o_ref,) = rest
    acc = h_ref[...]
    for i, (o, g) in enumerate(((oa, ga), (ob, gb), (oc, gc), (od, gd))):
        gv = g[...]
        mix = (o[...] * (gv * _sigmoid(gv))).astype(BF16)
        acc = acc + _dot(mix, w_ref[i * W_GRP:(i + 1) * W_GRP, :])
    if final:
        ms = jnp.mean(acc * acc, axis=-1, keepdims=True)
        acc = (acc * lax.rsqrt(ms + RMS_EPS)) * fg_ref[...]
    o_ref[...] = acc


def _outproj(outs, proj, h, w_out, layer, final_g=None):
    rows, d = h.shape
    tm = min(rows, 256)
    final = final_g is not None
    gate_blocks = [COL[k] // W_GRP for k in ("a_g", "b_g", "c_g", "d_g")]
    in_specs = [pl.BlockSpec((tm, W_GRP), lambda i: (i, 0)) for _ in range(4)]
    in_specs += [pl.BlockSpec((tm, W_GRP), functools.partial(lambda i, cb: (i, cb), cb=cb))
                 for cb in gate_blocks]
    in_specs += [pl.BlockSpec((tm, d), lambda i: (i, 0)),
                 pl.BlockSpec((None, 4 * W_GRP, d), lambda i: (layer, 0, 0))]
    args = list(outs) + [proj] * 4 + [h, w_out]
    if final:
        in_specs.append(pl.BlockSpec((1, d), lambda i: (0, 0)))
        args.append(final_g)
    return pl.pallas_call(
        functools.partial(_outproj_kernel, final=final),
        grid=(rows // tm,),
        in_specs=in_specs,
        out_specs=pl.BlockSpec((tm, d), lambda i: (i, 0)),
        out_shape=jax.ShapeDtypeStruct((rows, d), F32),
        compiler_params=_params(("parallel",), 48),
        name="outproj",
    )(*args)


SB_PAIRS = 2
SB_TILE = 256


def _sb_prompt_kernel(q_ref, k_ref, v_ref, o_ref, *, tq):
    qt = pl.program_id(2)
    lane = _iota((tq, LANES), 1)
    row = _iota((tq, tq), 0)
    col = _iota((tq, tq), 1)
    before = col < row
    upper = _upper_ones(tq)
    q = q_ref[0] * SCALE
    chains = []
    for pair in range(SB_PAIRS):
        qp = q[:, pair * LANES:(pair + 1) * LANES]
        for hh in range(2):
            in_head = (lane >= hh * HEAD_DIM) & (lane < (hh + 1) * HEAD_DIM)
            chains.append((pair, jnp.where(in_head, qp, 0.0).astype(BF16)))

    def tile(kt, state, diag):
        start = pl.multiple_of(kt * tq, tq)
        kv = [(k_ref[0, pl.ds(start, tq), p * LANES:(p + 1) * LANES].astype(BF16),
               v_ref[0, pl.ds(start, tq), p * LANES:(p + 1) * LANES].astype(BF16))
              for p in range(SB_PAIRS)]
        new = []
        for (pair, qm), (carry, acc) in zip(chains, state):
            k, v = kv[pair]
            z = _dot_nt(qm, k)
            lk = _log_keep(z)
            if diag:
                lk = jnp.where(before, lk, 0.0)
            between = _split_dot(lk, upper) + carry
            a = jnp.exp(z + lk + between)
            if diag:
                a = jnp.where(before, a, 0.0)
            new.append((carry + jnp.sum(lk, axis=1, keepdims=True), acc + _dot(a.astype(BF16), v)))
        return tuple(new)

    def any_live(state):
        top = functools.reduce(jnp.maximum, [jnp.max(carry) for carry, _ in state])
        return top > SB_DEAD_CARRY

    zero = (jnp.zeros((tq, 1), F32), jnp.zeros((tq, LANES), F32))
    state = tile(qt, (zero,) * len(chains), True)

    def cond(c):
        return (c[0] < qt) & c[1]

    def body(c):
        st = tile(qt - 1 - c[0], c[2], False)
        return c[0] + 1, any_live(st), st

    _, _, state = lax.while_loop(cond, body, (jnp.int32(0), any_live(state), state))
    for pair in range(SB_PAIRS):
        o_ref[0, :, pair * LANES:(pair + 1) * LANES] = jnp.where(
            lane < HEAD_DIM, state[2 * pair][1], state[2 * pair + 1][1])


def _sb_prompt(proj3):
    b, t, _ = proj3.shape
    tq = min(t, SB_TILE)
    width = SB_PAIRS * LANES
    qb, kb, vb = (COL[k] // width for k in ("a_q", "a_k", "a_v"))
    return pl.pallas_call(
        functools.partial(_sb_prompt_kernel, tq=tq),
        grid=(b, W_GRP // width, t // tq),
        in_specs=[pl.BlockSpec((1, tq, width), lambda bi, p, i: (bi, i, qb + p)),
                  pl.BlockSpec((1, t, width), lambda bi, p, i: (bi, 0, kb + p)),
                  pl.BlockSpec((1, t, width), lambda bi, p, i: (bi, 0, vb + p))],
        out_specs=pl.BlockSpec((1, tq, width), lambda bi, p, i: (bi, i, p)),
        out_shape=jax.ShapeDtypeStruct((b, t, W_GRP), F32),
        compiler_params=_params(("parallel", "parallel", "arbitrary")),
        name="sb_prompt",
    )(proj3, proj3, proj3)


def _cmp_kernel(xk_ref, xv_ref, wk_ref, wv_ref, ok_ref, ov_ref):
    ok_ref[...] = _dot(xk_ref[...].astype(BF16), wk_ref[...])
    ov_ref[...] = _dot(xv_ref[...].astype(BF16), wv_ref[...])


def _compress(xk, xv, wk_exp, wv_exp, layer, row0, rows):
    kdim = xk.shape[1]
    tr = min(rows, 256)
    off = row0 // tr
    return pl.pallas_call(
        _cmp_kernel,
        grid=(rows // tr,),
        in_specs=[pl.BlockSpec((tr, kdim), lambda i: (off + i, 0)),
                  pl.BlockSpec((tr, kdim), lambda i: (off + i, 0)),
                  pl.BlockSpec((None, kdim, LANES), lambda i: (layer, 0, 0)),
                  pl.BlockSpec((None, kdim, LANES), lambda i: (layer, 0, 0))],
        out_specs=[pl.BlockSpec((tr, LANES), lambda i: (i, 0))] * 2,
        out_shape=[jax.ShapeDtypeStruct((rows, LANES), F32)] * 2,
        compiler_params=_params(("parallel",), 48),
        name="compress",
    )(xk, xv, wk_exp, wv_exp)


def _cmp_cache_kernel(x_ref, w_ref, o_ref, *, rows):
    acc = jnp.zeros((rows, o_ref.shape[1]), F32)
    for d in range(HEAD_DIM):
        acc = acc + _dot(x_ref[pl.ds(d, rows, stride=HEAD_DIM), :].astype(BF16), w_ref[d])
    o_ref[...] = acc


CMP_CACHE_ROWS = 512


def _compress_cache(x, w_t, layer, n_rows):
    page = x.shape[1]
    n_out = w_t.shape[3]
    tr = CMP_CACHE_ROWS if n_rows % CMP_CACHE_ROWS == 0 else n_rows
    off = layer * (n_rows // tr)
    return pl.pallas_call(
        functools.partial(_cmp_cache_kernel, rows=tr),
        grid=(n_rows // tr,),
        in_specs=[pl.BlockSpec((tr * HEAD_DIM, page), lambda i: (off + i, 0)),
                  pl.BlockSpec((None, HEAD_DIM, page, n_out), lambda i: (layer, 0, 0, 0))],
        out_specs=pl.BlockSpec((tr, n_out), lambda i: (i, 0)),
        out_shape=jax.ShapeDtypeStruct((n_rows, n_out), F32),
        compiler_params=_params(("parallel",), 56),
        name="compress_cache",
    )(x, w_t)


def _head_slope(h):
    return 2.0 ** -(h + 1)


def _half_variant(x, x_swapped, kvh, par, low):
    src = x if kvh == par else x_swapped
    keep = low if par == 0 else jnp.logical_not(low)
    return jnp.where(keep, src, 0.0).astype(BF16)


def _nsa_cmp_kernel(q_ref, kc_ref, vc_ref, oc_ref, sel_ref, *, tq, ncmp):
    qt = pl.program_id(1)
    nsel = ncmp // 2
    ntop = min(N_SEL, nsel)
    half = LANES // 2

    def permuted(ref):
        ev = ref[0, pl.ds(0, nsel, stride=2), :]
        od = ref[0, pl.ds(1, nsel, stride=2), :]
        if nsel == half:
            return jnp.concatenate([ev, od], axis=0)
        pad = jnp.zeros((half - nsel, LANES), F32)
        return jnp.concatenate([ev, pad, od, pad], axis=0)

    kc = permuted(kc_ref)
    vc = permuted(vc_ref)
    kc_sw = pltpu.roll(kc, half, 1)
    vc_sw = pltpu.roll(vc, half, 1)
    low = _iota((LANES, LANES), 1) < half

    n_i = _iota((tq, LANES), 1)
    odd = n_i >= half
    slot = jnp.where(odd, n_i - half, n_i)
    cmp_blk = jnp.where(odd, 2 * slot + 1, 2 * slot)
    qpos = qt * tq + _iota((tq, LANES), 0)
    dist = (qpos - (cmp_blk * L_CMP + (L_CMP - 1))).astype(F32)
    ok = (dist >= 0) & (slot < nsel)

    q = q_ref[0] * SCALE
    o_pairs = [jnp.zeros((tq, LANES), F32) for _ in range(4)]
    for kvh in range(2):
        imp = jnp.zeros((tq, LANES), F32)
        for g in range(4):
            h = kvh * 4 + g
            par, pair = h % 2, h // 2
            qb = q[:, pair * LANES:(pair + 1) * LANES].astype(BF16)
            z = _dot_nt(qb, _half_variant(kc, kc_sw, kvh, par, low))
            s = jnp.where(ok, z - _head_slope(h) * dist, NEG_BIG)
            e = jnp.exp(s - jnp.max(s, axis=1, keepdims=True))
            p = jnp.where(ok, e / jnp.sum(e, axis=1, keepdims=True), 0.0)
            imp = imp + p
            o_pairs[pair] = o_pairs[pair] + _dot(p.astype(BF16), _half_variant(vc, vc_sw, kvh, par, low))
        imp_sel = imp + pltpu.roll(imp, half, 1)
        blk = n_i
        cur = qpos >> 6
        forced = (blk == 0) | (blk == cur) | (blk == cur - 1)
        valid = (blk * L_SEL <= qpos) & (blk < nsel)
        score = jnp.where(valid, imp_sel + jnp.where(forced, FORCE_BONUS, 0.0), -jnp.inf)
        def ranked(score=score, valid=valid):
            rank = jnp.zeros((tq, LANES), F32)
            for i in range(nsel):
                si = score[:, i:i + 1]
                beats = (si > score) | ((si == score) & (blk > i))
                rank = rank + jnp.where(beats, 1.0, 0.0)
            return jnp.where(valid & (rank < ntop), 1.0, 0.0)

        sel_ref[0, kvh] = lax.cond((qt + 1) * tq > ntop * L_SEL, ranked,
                                   lambda valid=valid: jnp.where(valid, 1.0, 0.0))
    oc_ref[0] = jnp.concatenate(o_pairs, axis=1)


def _nsa_cmp_prompt(proj3, kcmp, vcmp):
    b, t, _ = proj3.shape
    ncmp = kcmp.shape[1]
    tq = min(t, 256)
    qb = COL["b_q"] // W_GRP
    return pl.pallas_call(
        functools.partial(_nsa_cmp_kernel, tq=tq, ncmp=ncmp),
        grid=(b, t // tq),
        in_specs=[pl.BlockSpec((1, tq, W_GRP), lambda bi, i: (bi, i, qb)),
                  pl.BlockSpec((1, ncmp, LANES), lambda bi, i: (bi, 0, 0)),
                  pl.BlockSpec((1, ncmp, LANES), lambda bi, i: (bi, 0, 0))],
        out_specs=[pl.BlockSpec((1, tq, W_GRP), lambda bi, i: (bi, i, 0)),
                   pl.BlockSpec((1, 2, tq, LANES), lambda bi, i: (bi, 0, i, 0))],
        out_shape=[jax.ShapeDtypeStruct((b, t, W_GRP), F32),
                   jax.ShapeDtypeStruct((b, 2, t, LANES), F32)],
        compiler_params=_params(("parallel", "parallel")),
        name="nsa_cmp_prompt",
    )(proj3, kcmp, vcmp)


def _nsa_att_kernel(q_ref, ks_ref, vs_ref, kw_ref, vw_ref, sel_ref, br_ref, oc_ref, o_ref,
                    ksv, vsv, kwv, vwv, *, tq, t):
    qt = pl.program_id(1)

    @pl.when(qt == 0)
    def _():
        low = _iota((t, LANES), 1) < LANES // 2
        for src, dst in ((ks_ref, ksv), (vs_ref, vsv), (kw_ref, kwv), (vw_ref, vwv)):
            x = src[0]
            xs = pltpu.roll(x, LANES // 2, 1)
            for kvh in range(2):
                for par in range(2):
                    dst[kvh * 2 + par] = _half_variant(x, xs, kvh, par, low)

    q = q_ref[0] * SCALE
    sig = _sigmoid(br_ref[0])
    oc = oc_ref[0]
    rel = (_iota((tq, tq), 0) - _iota((tq, tq), 1)).astype(F32)
    lane = _iota((tq, LANES), 1)
    blk_row = _iota((LANES, tq), 0)
    blk_col = _iota((LANES, tq), 1)
    init1 = (jnp.full((tq, 1), NEG_BIG, F32), jnp.zeros((tq, 1), F32), jnp.zeros((tq, LANES), F32))
    n_win = jnp.minimum(qt, (WINDOW + tq - 1) // tq) + 1

    for kvh in range(2):
        heads = [kvh * 4 + g for g in range(4)]
        qbs = [q[:, (h // 2) * LANES:(h // 2 + 1) * LANES].astype(BF16) for h in heads]
        selk = sel_ref[0, kvh].astype(BF16)

        def tile(kt, state, kv, vv, selected, kvh=kvh, heads=heads, qbs=qbs, selk=selk):
            start = pl.multiple_of(kt * tq, tq)
            dist = rel + ((qt - kt) * tq).astype(F32)
            if selected:
                expand = jnp.where(blk_row == ((start + blk_col) >> 6), 1.0, 0.0).astype(BF16)
                mask = (_dot(selk, expand) > 0.5) & (dist >= 0)
            else:
                mask = (dist >= 0) & (dist < WINDOW)
            bias = jnp.where(mask, 0.0, NEG_BIG)
            kvs = [(kv[kvh * 2 + par, pl.ds(start, tq), :], vv[kvh * 2 + par, pl.ds(start, tq), :])
                   for par in range(2)]
            new = []
            for h, qb, (m, l, acc) in zip(heads, qbs, state):
                k, v = kvs[h % 2]
                s = (_dot_nt(qb, k) - _head_slope(h) * dist) + bias
                m_new = jnp.maximum(m, jnp.max(s, axis=1, keepdims=True))
                p = jnp.exp(s - m_new)
                alpha = jnp.exp(m - m_new)
                new.append((m_new, alpha * l + jnp.sum(p, axis=1, keepdims=True),
                            alpha * acc + _dot(p.astype(BF16), v)))
            return tuple(new)

        init = (init1,) * len(heads)
        st_s = lax.fori_loop(0, qt + 1, lambda i, st: tile(qt - i, st, ksv, vsv, True), init)
        st_w = lax.fori_loop(0, n_win, lambda i, st: tile(qt - i, st, kwv, vwv, False), init)
        o_heads = []
        for i, h in enumerate(heads):
            g_c = sig[:, 3 * h:3 * h + 1]
            g_s = sig[:, 3 * h + 1:3 * h + 2]
            g_w = sig[:, 3 * h + 2:3 * h + 3]
            oc_pair = oc[:, (h // 2) * LANES:(h // 2 + 1) * LANES]
            o_heads.append(g_c * oc_pair + g_s * (st_s[i][2] / st_s[i][1])
                           + g_w * (st_w[i][2] / st_w[i][1]))
        for j in range(2):
            pair = kvh * 2 + j
            o_ref[0, :, pair * LANES:(pair + 1) * LANES] = jnp.where(
                lane < HEAD_DIM, o_heads[2 * j], o_heads[2 * j + 1])


def _nsa_att_prompt(proj3, sel, oc):
    b, t, _ = proj3.shape
    tq = min(t, 256)
    qb = COL["b_q"] // W_GRP
    kv_blocks = [COL[k] // LANES for k in ("b_ks", "b_vs", "b_kw", "b_vw")]
    br = COL["b_br"] // LANES
    in_specs = [pl.BlockSpec((1, tq, W_GRP), lambda bi, i: (bi, i, qb))]
    in_specs += [pl.BlockSpec((1, t, LANES), functools.partial(lambda bi, i, cb: (bi, 0, cb), cb=cb))
                 for cb in kv_blocks]
    in_specs += [pl.BlockSpec((1, 2, tq, LANES), lambda bi, i: (bi, 0, i, 0)),
                 pl.BlockSpec((1, tq, LANES), lambda bi, i: (bi, i, br)),
                 pl.BlockSpec((1, tq, W_GRP), lambda bi, i: (bi, i, 0))]
    return pl.pallas_call(
        functools.partial(_nsa_att_kernel, tq=tq, t=t),
        grid=(b, t // tq),
        in_specs=in_specs,
        out_specs=pl.BlockSpec((1, tq, W_GRP), lambda bi, i: (bi, i, 0)),
        out_shape=jax.ShapeDtypeStruct((b, t, W_GRP), F32),
        scratch_shapes=[pltpu.VMEM((4, t, LANES), BF16)] * 4,
        compiler_params=_params(("parallel", "arbitrary"), 48),
        name="nsa_att_prompt",
    )(proj3, proj3, proj3, proj3, proj3, sel, proj3, oc)


def _pool_kernel(*refs, tt, start, has_halo):
    if has_halo:
        u_ref, halo_ref, pre_ref, w_ref, sc_ref, o_ref, e_ref, d_ref = refs
    else:
        u_ref, pre_ref, w_ref, sc_ref, o_ref, e_ref, d_ref = refs
    ti = pl.program_id(1)
    e_ref[POOL_HALO:POOL_HALO + tt, :] = u_ref[0]
    if has_halo:
        @pl.when(ti == 0)
        def _():
            e_ref[0:POOL_HALO, :] = pre_ref[0]

        @pl.when(ti > 0)
        def _():
            e_ref[0:POOL_HALO, :] = halo_ref[0]
    else:
        e_ref[0:POOL_HALO, :] = pre_ref[0]

    ch = min(tt, 64)
    for c in range(tt // ch):
        r0 = c * ch
        pos = start + ti * tt + r0 + _iota((ch, LANES), 0)
        for g, win in enumerate(POOL_WINDOWS):
            lanes = slice(g * LANES, (g + 1) * LANES)
            cur = e_ref[POOL_HALO + r0:POOL_HALO + r0 + ch, lanes]
            total = cur
            for s in range(1, win):
                total = total + e_ref[POOL_HALO + r0 - s:POOL_HALO + r0 - s + ch, lanes]
            cnt = jnp.minimum(win, pos + 1).astype(F32)
            d_ref[r0:r0 + ch, lanes] = total / cnt - cur
    for g in range(len(POOL_WINDOWS)):
        lanes = slice(g * LANES, (g + 1) * LANES)
        y = _dot(d_ref[:, lanes].astype(BF16), w_ref[g])
        o_ref[0, :, lanes] = y * sc_ref[:, lanes]


def _pool(u_arr, cb, prefix, w_pool, pool_scale, layer, start):
    b, t, _ = u_arr.shape
    tt = min(t, 512)
    nt = t // tt
    has_halo = nt > 1
    in_specs = [pl.BlockSpec((1, tt, W_GRP), lambda bi, i: (bi, i, cb))]
    args = [u_arr]
    if has_halo:
        per = tt // POOL_HALO
        in_specs.append(pl.BlockSpec((1, POOL_HALO, W_GRP),
                                     lambda bi, i: (bi, jnp.maximum(i * per - 1, 0), cb)))
        args.append(u_arr)
    in_specs += [pl.BlockSpec((1, POOL_HALO, W_GRP), lambda bi, i: (bi, 0, 0)),
                 pl.BlockSpec((None, 4, LANES, LANES), lambda bi, i: (layer, 0, 0, 0)),
                 pl.BlockSpec((None, 1, W_GRP), lambda bi, i: (layer, 0, 0))]
    args += [prefix, w_pool, pool_scale]
    return pl.pallas_call(
        functools.partial(_pool_kernel, tt=tt, start=start, has_halo=has_halo),
        grid=(b, nt),
        in_specs=in_specs,
        out_specs=pl.BlockSpec((1, tt, W_GRP), lambda bi, i: (bi, i, 0)),
        out_shape=jax.ShapeDtypeStruct((b, t, W_GRP), F32),
        scratch_shapes=[pltpu.VMEM((POOL_HALO + tt, W_GRP), F32), pltpu.VMEM((tt, W_GRP), F32)],
        compiler_params=_params(("parallel", "arbitrary")),
        name="pool",
    )(*args)


def _conv_kernel(*refs, tt, has_halo):
    if has_halo:
        (a_ref, g_ref, ha_ref, hg_ref, pre_ref, cw_ref, cb_ref, lg_ref, lb_ref, pw_ref,
         o_ref, u_ref, e_ref, y_ref) = refs
    else:
        (a_ref, g_ref, pre_ref, cw_ref, cb_ref, lg_ref, lb_ref, pw_ref,
         o_ref, u_ref, e_ref, y_ref) = refs
    ti = pl.program_id(1)
    u = a_ref[0] * _sigmoid(g_ref[0])
    u_ref[0] = u
    e_ref[0, CONV_HALO:CONV_HALO + tt, :] = u
    if has_halo:
        @pl.when(ti == 0)
        def _():
            e_ref[0, 0:CONV_HALO, :] = pre_ref[0]

        @pl.when(ti > 0)
        def _():
            e_ref[0, 0:CONV_HALO, :] = ha_ref[0] * _sigmoid(hg_ref[0])
    else:
        e_ref[0, 0:CONV_HALO, :] = pre_ref[0]

    n_shift = CONV_HALO + tt - SUBLANES
    for s in range(1, SUBLANES):
        for r in range(0, n_shift, 64):
            rows = min(64, n_shift - r)
            e_ref[s, r:r + rows, :] = e_ref[0, r + s:r + s + rows, :]

    ch = min(tt, 32)
    lead = CONV_HALO - (CONV_W - 1)
    for c in range(tt // ch):
        r0 = c * ch
        acc = jnp.broadcast_to(cb_ref[...], (ch, W_GRP))
        for j in range(CONV_W):
            s = (lead + j) % SUBLANES
            base = r0 + lead + j - s
            acc = acc + cw_ref[j:j + 1, :] * e_ref[s, base:base + ch, :]
        y_ref[r0:r0 + ch, :] = acc
    y = y_ref[...]
    yc = y - jnp.mean(y, axis=-1, keepdims=True)
    yn = yc * lax.rsqrt(jnp.mean(yc * yc, axis=-1, keepdims=True) + LN_EPS)
    yn = yn * lg_ref[...] + lb_ref[...]
    o_ref[0] = _dot((yn * _sigmoid(yn)).astype(BF16), pw_ref[...])


def _conv(arr, cb_a, cb_g, prefix, conv_w, conv_b, ln_g, ln_b, w_pw, layer):
    b, t, _ = arr.shape
    tt = min(t, 512)
    nt = t // tt
    has_halo = nt > 1
    in_specs = [pl.BlockSpec((1, tt, W_GRP), lambda bi, i: (bi, i, cb_a)),
                pl.BlockSpec((1, tt, W_GRP), lambda bi, i: (bi, i, cb_g))]
    args = [arr, arr]
    if has_halo:
        per = tt // CONV_HALO
        in_specs += [pl.BlockSpec((1, CONV_HALO, W_GRP),
                                  lambda bi, i: (bi, jnp.maximum(i * per - 1, 0), cb_a)),
                     pl.BlockSpec((1, CONV_HALO, W_GRP),
                                  lambda bi, i: (bi, jnp.maximum(i * per - 1, 0), cb_g))]
        args += [arr, arr]
    vec = lambda: pl.BlockSpec((None, 1, W_GRP), lambda bi, i: (layer, 0, 0))
    in_specs += [pl.BlockSpec((1, CONV_HALO, W_GRP), lambda bi, i: (bi, 0, 0)),
                 pl.BlockSpec((None, CONV_HALO, W_GRP), lambda bi, i: (layer, 0, 0)),
                 vec(), vec(), vec(),
                 pl.BlockSpec((None, W_GRP, W_GRP), lambda bi, i: (layer, 0, 0))]
    args += [prefix, conv_w, conv_b, ln_g, ln_b, w_pw]
    return pl.pallas_call(
        functools.partial(_conv_kernel, tt=tt, has_halo=has_halo),
        grid=(b, nt),
        in_specs=in_specs,
        out_specs=[pl.BlockSpec((1, tt, W_GRP), lambda bi, i: (bi, i, 0))] * 2,
        out_shape=[jax.ShapeDtypeStruct((b, t, W_GRP), F32)] * 2,
        scratch_shapes=[pltpu.VMEM((SUBLANES, CONV_HALO + tt, W_GRP), F32),
                        pltpu.VMEM((tt, W_GRP), F32)],
        compiler_params=_params(("parallel", "arbitrary"), 48),
        name="conv",
    )(*args)


SB_ROWS = 32
PAGES_PER_STEP = 16


def _sb_sample_kernel(pt_ref, q_ref, kn_ref, vn_ref, k_hbm, v_hbm, o_ref,
                      kbuf, vbuf, sem, carry_ref, acc_ref, *, layer, n_pages, page):
    b = pl.program_id(0)
    qbd = q_ref[0].astype(BF16)
    upper = _upper_ones(page)

    def page_copies(i, slot):
        pg = pt_ref[b * n_pages + (n_pages - 1 - i)]
        return (pltpu.make_async_copy(k_hbm.at[layer, pg], kbuf.at[slot], sem.at[0, slot]),
                pltpu.make_async_copy(v_hbm.at[layer, pg], vbuf.at[slot], sem.at[1, slot]))

    def start(i, slot):
        for c in page_copies(i, slot):
            c.start()

    def wait(i, slot):
        for c in page_copies(i, slot):
            c.wait()

    start(0, 0)

    def absorb(k, v, mask, channel_major):
        kb, vb = k.astype(BF16), v.astype(BF16)
        z = _dot(qbd, kb) if channel_major else _dot_nt(qbd, kb)
        lk = _log_keep(z)
        if mask is not None:
            lk = jnp.where(mask, lk, 0.0)
        between = _split_dot(lk, upper) + carry_ref[...]
        a = jnp.exp(z + lk + between)
        if mask is not None:
            a = jnp.where(mask, a, 0.0)
        ab = a.astype(BF16)
        acc_ref[...] += _dot_nt(ab, vb) if channel_major else _dot(ab, vb)
        carry_ref[...] += jnp.sum(lk, axis=1, keepdims=True)

    carry_ref[...] = jnp.zeros_like(carry_ref)
    acc_ref[...] = jnp.zeros_like(acc_ref)
    pad = jnp.zeros((page - kn_ref.shape[1], W_GRP), F32)
    r = _iota((SB_ROWS, page), 0)
    j = _iota((SB_ROWS, page), 1)
    absorb(jnp.concatenate([kn_ref[0], pad], axis=0),
           jnp.concatenate([vn_ref[0], pad], axis=0), j < (r >> 3), False)

    def live():
        return jnp.max(carry_ref[...]) > SB_DEAD_CARRY

    def body(c):
        i = c[0]
        slot = i & 1
        wait(i, slot)

        @pl.when(i + 1 < n_pages)
        def _():
            start(i + 1, 1 - slot)

        absorb(kbuf[slot].reshape(W_GRP, page), vbuf[slot].reshape(W_GRP, page), None, True)
        return i + 1, live()

    i_end, _ = lax.while_loop(lambda c: (c[0] < n_pages) & c[1], body, (jnp.int32(0), live()))

    @pl.when(i_end < n_pages)
    def _():
        wait(i_end, i_end & 1)

    acc = acc_ref[...]
    r = _iota((SB_ROWS, HEAD_DIM), 0)
    out = jnp.zeros((SB_ROWS, HEAD_DIM), F32)
    for h in range(W_GRP // HEAD_DIM):
        out = jnp.where((r & 7) == h, acc[:, h * HEAD_DIM:(h + 1) * HEAD_DIM], out)
    o_ref[0] = out


def _sb_sample(page_table, qbd, k_new, v_new, cache_k, cache_v, layer):
    db, n_pages = page_table.shape
    n_heads, page = cache_k.shape[2], cache_k.shape[4]
    small = lambda r: pl.BlockSpec((1, r, W_GRP), lambda bi, pt: (bi, 0, 0))
    grid_spec = pltpu.PrefetchScalarGridSpec(
        num_scalar_prefetch=1,
        grid=(db,),
        in_specs=[small(SB_ROWS), small(k_new.shape[1]), small(v_new.shape[1]),
                  pl.BlockSpec(memory_space=pl.ANY), pl.BlockSpec(memory_space=pl.ANY)],
        out_specs=pl.BlockSpec((1, SB_ROWS, HEAD_DIM), lambda bi, pt: (bi, 0, 0)),
        scratch_shapes=[pltpu.VMEM((2, n_heads, HEAD_DIM, page), F32),
                        pltpu.VMEM((2, n_heads, HEAD_DIM, page), F32),
                        pltpu.SemaphoreType.DMA((2, 2)),
                        pltpu.VMEM((SB_ROWS, 1), F32), pltpu.VMEM((SB_ROWS, W_GRP), F32)],
    )
    return pl.pallas_call(
        functools.partial(_sb_sample_kernel, layer=layer, n_pages=n_pages, page=page),
        grid_spec=grid_spec,
        out_shape=jax.ShapeDtypeStruct((db, SB_ROWS, HEAD_DIM), F32),
        compiler_params=_params(("arbitrary",)),
        name="sb_sample",
    )(page_table.reshape(-1), qbd, k_new, v_new, cache_k, cache_v)


def _row_fields(shape):
    r = _iota(shape, 0)
    qi = (r >> 1) & 3
    kvh = r & 1
    h = kvh * 4 + (r >> 3)
    slope = jnp.zeros(shape, F32)
    for hh in range(8):
        slope = jnp.where(h == hh, _head_slope(hh), slope)
    return qi, kvh, slope


def _pick_half(x, kvh64):
    return jnp.where(kvh64 == 0, x[:, :HEAD_DIM], x[:, HEAD_DIM:])


def _nsa_small_kernel(pt_ref, q_ref, kall_ref, vall_ref, kwc_ref, vwc_ref, kwn_ref, vwn_ref,
                      oc_ref, ow_ref, sel_ref, used_ref, kg_ref, vg_ref, *, n_pages, past):
    b = pl.program_id(0)

    def gather(p, carry):
        idx = pt_ref[b * n_pages + p]
        kg_ref[pl.ds(p, 1), :] = kall_ref[pl.ds(idx, 1), :]
        vg_ref[pl.ds(p, 1), :] = vall_ref[pl.ds(idx, 1), :]
        return carry

    lax.fori_loop(0, n_pages, gather, 0)

    q = q_ref[0].astype(BF16)
    rows = q_ref.shape[1]
    _, kvh64, _ = _row_fields((rows, HEAD_DIM))

    qi, _, slope = _row_fields((rows, n_pages))
    pidx = _iota((rows, n_pages), 1)
    qpos = past + qi
    per_page = kg_ref.shape[1] // LANES
    scores, oks = [], []
    for n4 in range(per_page):
        z = _dot_nt(q, kg_ref[:, n4 * LANES:(n4 + 1) * LANES].astype(BF16))
        dist = (qpos - ((pidx * per_page + n4) * L_CMP + (L_CMP - 1))).astype(F32)
        ok = dist >= 0
        oks.append(ok)
        scores.append(jnp.where(ok, z - slope * dist, NEG_BIG))
    m = functools.reduce(jnp.maximum, [jnp.max(s, axis=1, keepdims=True) for s in scores])
    es = [jnp.exp(s - m) for s in scores]
    tot = functools.reduce(jnp.add, [jnp.sum(e, axis=1, keepdims=True) for e in es])
    ps = [jnp.where(ok, e / tot, 0.0) for ok, e in zip(oks, es)]
    o_c = functools.reduce(jnp.add, [
        _dot(p.astype(BF16), vg_ref[:, n4 * LANES:(n4 + 1) * LANES].astype(BF16))
        for n4, p in enumerate(ps)])
    oc_ref[0] = _pick_half(o_c, kvh64)

    def group_sum(x):
        return x[0:8] + x[8:16] + x[16:24] + x[24:32]

    lane = _iota((8, n_pages), 1)
    s_even = group_sum(ps[0] + ps[1]) + jnp.where(lane == 0, FORCE_BONUS, 0.0)
    s_odd = group_sum(ps[2] + ps[3]) + jnp.where(lane == n_pages - 1, FORCE_BONUS, 0.0)
    score = jnp.concatenate([s_even, s_odd], axis=1)
    lane2 = _iota((8, 2 * n_pages), 1)
    blk = jnp.where(lane2 < n_pages, 2 * lane2, 2 * (lane2 - n_pages) + 1)
    sel = jnp.zeros((8, 2 * n_pages), F32)
    for _ in range(N_SEL - 1):
        best = jnp.max(score, axis=1, keepdims=True)
        first = jnp.min(jnp.where(score == best, blk, 2 ** 30), axis=1, keepdims=True)
        hit = (blk == first) & (best > -jnp.inf)
        sel = jnp.where(hit, 1.0, sel)
        score = jnp.where(blk == first, -jnp.inf, score)
    sel_ref[0] = sel
    any_row = jnp.max(sel, axis=0, keepdims=True)
    used = jnp.maximum(any_row[:, :n_pages], any_row[:, n_pages:])
    used_ref[0] = jnp.broadcast_to(used, (8, n_pages)).astype(jnp.int32)

    wbuf = kwc_ref.shape[4]
    n_new = kwn_ref.shape[1]
    padw = jnp.zeros((LANES - n_new, LANES), F32)
    kw_new = jnp.concatenate([kwn_ref[0], padw], axis=0).astype(BF16)
    vw_new = jnp.concatenate([vwn_ref[0], padw], axis=0).astype(BF16)
    kw_old = kwc_ref[0, 0].reshape(LANES, wbuf).astype(BF16)
    vw_old = vwc_ref[0, 0].reshape(LANES, wbuf).astype(BF16)

    def scores(z, dist):
        ok = (dist >= 0) & (dist < WINDOW)
        return jnp.where(ok, z - slope_of(dist.shape) * dist.astype(F32), NEG_BIG), ok

    def slope_of(shape):
        return _row_fields(shape)[2]

    qi_o, _, _ = _row_fields((rows, wbuf))
    qi_n, _, _ = _row_fields((rows, LANES))
    s_o, ok_o = scores(_dot(q, kw_old), wbuf + qi_o - _iota((rows, wbuf), 1))
    s_n, ok_n = scores(_dot_nt(q, kw_new), qi_n - _iota((rows, LANES), 1))
    m = jnp.maximum(jnp.max(s_o, axis=1, keepdims=True), jnp.max(s_n, axis=1, keepdims=True))
    e_o, e_n = jnp.exp(s_o - m), jnp.exp(s_n - m)
    tot = jnp.sum(e_o, axis=1, keepdims=True) + jnp.sum(e_n, axis=1, keepdims=True)
    p_o = jnp.where(ok_o, e_o / tot, 0.0).astype(BF16)
    p_n = jnp.where(ok_n, e_n / tot, 0.0).astype(BF16)
    ow_ref[0] = _pick_half(_dot_nt(p_o, vw_old) + _dot(p_n, vw_new), kvh64)


def _nsa_small(page_table, q_rows, kall, vall, win_k, win_v, kw_new, vw_new, layer, past):
    db, n_pages = page_table.shape
    rows = q_rows.shape[1]
    wbuf = win_k.shape[4]
    whole = lambda a: pl.BlockSpec(a.shape, lambda bi, pt: (0,) * a.ndim)
    per_b = lambda a: pl.BlockSpec((1,) + a.shape[1:], lambda bi, pt: (bi,) + (0,) * (a.ndim - 1))
    win = pl.BlockSpec((1, 1, 2, HEAD_DIM, wbuf), lambda bi, pt: (layer, bi, 0, 0, 0))
    grid_spec = pltpu.PrefetchScalarGridSpec(
        num_scalar_prefetch=1,
        grid=(db,),
        in_specs=[per_b(q_rows), whole(kall), whole(vall), win, win, per_b(kw_new), per_b(vw_new)],
        out_specs=[pl.BlockSpec((1, rows, HEAD_DIM), lambda bi, pt: (bi, 0, 0)),
                   pl.BlockSpec((1, rows, HEAD_DIM), lambda bi, pt: (bi, 0, 0)),
                   pl.BlockSpec((1, 8, 2 * n_pages), lambda bi, pt: (bi, 0, 0)),
                   pl.BlockSpec((1, 8, n_pages), lambda bi, pt: (bi, 0, 0))],
        scratch_shapes=[pltpu.VMEM((n_pages, kall.shape[1]), F32)] * 2,
    )
    return pl.pallas_call(
        functools.partial(_nsa_small_kernel, n_pages=n_pages, past=past),
        grid_spec=grid_spec,
        out_shape=[jax.ShapeDtypeStruct((db, rows, HEAD_DIM), F32),
                   jax.ShapeDtypeStruct((db, rows, HEAD_DIM), F32),
                   jax.ShapeDtypeStruct((db, 8, 2 * n_pages), F32),
                   jax.ShapeDtypeStruct((db, 8, n_pages), jnp.int32)],
        compiler_params=_params(("arbitrary",)),
        name="nsa_small_sample",
    )(page_table.reshape(-1), q_rows, kall, vall, win_k, win_v, kw_new, vw_new)


def _nsa_sel_kernel(pt_ref, used_ref, q_ref, sel_ref, kn_ref, vn_ref, oc_ref, ow_ref, br_ref, *rest,
                    npg, n_pages, page, past):
    del pt_ref
    krefs, vrefs = rest[:npg], rest[npg:2 * npg]
    o_ref, m_ref, l_ref, acc_ref = rest[2 * npg:]
    b = pl.program_id(0)
    step = pl.program_id(1)
    q = q_ref[0].astype(BF16)
    rows = q_ref.shape[1]
    qi, _, slope = _row_fields((rows, page))
    lane = _iota((rows, page), 1)
    qpos = past + qi

    def absorb(k, v, kpos0, mask, channel_major):
        kb, vb = k.astype(BF16), v.astype(BF16)
        dist = (qpos - (kpos0 + lane)).astype(F32)
        s = (_dot(q, kb) if channel_major else _dot_nt(q, kb)) - slope * dist
        if mask is None:
            mask = dist >= 0
        m_old = m_ref[...]
        m_new = jnp.maximum(m_old, jnp.max(jnp.where(mask, s, NEG_BIG), axis=1, keepdims=True))
        p = jnp.where(mask, jnp.exp(s - m_new), 0.0)
        alpha = jnp.exp(m_old - m_new)
        l_ref[...] = alpha * l_ref[...] + jnp.sum(p, axis=1, keepdims=True)
        pb = p.astype(BF16)
        acc_ref[...] = alpha * acc_ref[...] + (_dot_nt(pb, vb) if channel_major else _dot(pb, vb))
        m_ref[...] = m_new

    @pl.when(step == 0)
    def _():
        m_ref[...] = jnp.full_like(m_ref, NEG_BIG)
        l_ref[...] = jnp.zeros_like(l_ref)
        acc_ref[...] = jnp.zeros_like(acc_ref)
        pad = jnp.zeros((page - kn_ref.shape[1], LANES), F32)
        absorb(jnp.concatenate([kn_ref[0], pad], axis=0),
               jnp.concatenate([vn_ref[0], pad], axis=0), past, None, False)

    selm = jnp.concatenate([sel_ref[0]] * (rows // 8), axis=0)
    pl_lane = _iota((rows, n_pages), 1)
    per_blk = page // L_SEL
    for g in range(npg):
        pg = n_pages - 1 - (step * npg + g)

        @pl.when(used_ref[b * n_pages + pg] > 0)
        def _(g=g, pg=pg):
            mask = jnp.zeros((rows, page), jnp.bool_)
            for j in range(per_blk):
                cj = jnp.sum(jnp.where(pl_lane == pg, selm[:, j * n_pages:(j + 1) * n_pages], 0.0),
                             axis=1, keepdims=True)
                mask = mask | ((cj > 0.5) & (lane >= j * L_SEL) & (lane < (j + 1) * L_SEL))
            absorb(krefs[g][0, 0].reshape(LANES, page), vrefs[g][0, 0].reshape(LANES, page),
                   pg * page, mask, True)

    @pl.when(step == pl.num_programs(1) - 1)
    def _():
        _, kvh64, _ = _row_fields((rows, HEAD_DIM))
        o_s = _pick_half(acc_ref[...] / l_ref[...], kvh64)
        sig = _sigmoid(br_ref[0])
        o_ref[0] = sig[:, 0:1] * oc_ref[0] + sig[:, 1:2] * o_s + sig[:, 2:3] * ow_ref[0]


def _nsa_sel_sample(page_table, used, q_rows, sel, k_new, v_new, oc, ow, br_rows, cache_k, cache_v,
                    layer, past):
    db, n_pages = page_table.shape
    rows = q_rows.shape[1]
    page = cache_k.shape[4]
    npg = min(PAGES_PER_STEP, n_pages)
    steps = n_pages // npg

    def page_spec(g):
        def index(bi, s, pt, us):
            return (layer, pt[bi * n_pages + (n_pages - 1 - (s * npg + g))], 0, 0, 0)
        return pl.BlockSpec((1, 1, 2, HEAD_DIM, page), index)

    per_b = lambda a: pl.BlockSpec((1,) + a.shape[1:],
                                   lambda bi, s, pt, us: (bi,) + (0,) * (a.ndim - 1))
    grid_spec = pltpu.PrefetchScalarGridSpec(
        num_scalar_prefetch=2,
        grid=(db, steps),
        in_specs=[per_b(a) for a in (q_rows, sel, k_new, v_new, oc, ow, br_rows)]
        + [page_spec(g) for g in range(npg)] * 2,
        out_specs=pl.BlockSpec((1, rows, HEAD_DIM), lambda bi, s, pt, us: (bi, 0, 0)),
        scratch_shapes=[pltpu.VMEM((rows, 1), F32), pltpu.VMEM((rows, 1), F32),
                        pltpu.VMEM((rows, LANES), F32)],
    )
    return pl.pallas_call(
        functools.partial(_nsa_sel_kernel, npg=npg, n_pages=n_pages, page=page, past=past),
        grid_spec=grid_spec,
        out_shape=jax.ShapeDtypeStruct((db, rows, HEAD_DIM), F32),
        compiler_params=_params(("parallel", "arbitrary")),
        name="nsa_sel_sample",
    )(page_table.reshape(-1), used.reshape(-1), q_rows, sel, k_new, v_new, oc, ow, br_rows,
      *([cache_k] * npg), *([cache_v] * npg))


def _pack_w_in(w_in):
    depth, d, _ = w_in.shape
    wt = jnp.transpose(w_in, (0, 2, 1))
    pieces, src = [], 0
    placed = {}
    for name, width in _SRC:
        placed[name] = (src, width)
        src += width
    order = sorted(COL, key=COL.get)
    pos = 0
    for name in order:
        assert COL[name] == pos
        s, width = placed[name]
        pieces.append(wt[:, s:s + width, :])
        pos += width
    pieces.append(jnp.zeros((depth, N_PACK - pos, d), w_in.dtype))
    return jnp.concatenate(pieces, axis=1).astype(BF16)


def _expand_cmp_weight(w):
    depth, l, d, e = w.shape
    eye = jnp.eye(2, dtype=w.dtype)
    return jnp.einsum("zlde,hg->zlhdge", w, eye).reshape(depth, l * 2 * d, 2 * e).astype(BF16)


def _expand_cmp_weight_t(w, page):
    depth, l, d, e = w.shape
    n = page // l
    eye = jnp.eye(n, dtype=w.dtype)
    return jnp.einsum("zlde,nm->zdnlme", w, eye).reshape(depth, d, page, n * e).astype(BF16)


def _sample_rows(x, n_heads_last):
    db, tq = x.shape[:2]
    x = x.reshape(db, tq, 2, 4, n_heads_last)
    return x.transpose(0, 3, 1, 2, 4).reshape(db, 4 * tq * 2, n_heads_last)


def kernel(x_prompt, x_sample, cache_sb_k, cache_sb_v, cache_cmp_k, cache_cmp_v, cache_sel_k, cache_sel_v, cache_win_k, cache_win_v, state_pool, state_conv, page_table, norm_g, w_in, w_cmp_k, w_cmp_v, w_pool, pool_scale, conv_w, conv_b, ln_g, ln_b, w_pw, w_out, final_g):
    bp, seq, d_model = x_prompt.shape
    db, tq_s, _ = x_sample.shape
    depth, n_pool, page = cache_sb_k.shape[:3]
    assert tq_s == 4 and page == LANES and cache_win_k.shape[2] == WINDOW and seq % 256 == 0

    w = _prepare_weights(norm_g, w_in, w_cmp_k, w_cmp_v, w_pool, pool_scale, conv_w, conv_b,
                         ln_g, ln_b, w_pw, w_out, final_g)
    chan_major = lambda c: jnp.transpose(c, (0, 1, 3, 4, 2))
    sbk, sbv = chan_major(cache_sb_k), chan_major(cache_sb_v)
    selk, selv = chan_major(cache_sel_k), chan_major(cache_sel_v)
    cmpk = chan_major(cache_cmp_k).reshape(depth * n_pool * 2 * HEAD_DIM, page)
    cmpv = chan_major(cache_cmp_v).reshape(depth * n_pool * 2 * HEAD_DIM, page)

    hp = x_prompt.reshape(bp * seq, d_model)
    hs = x_sample.reshape(db * tq_s, d_model)
    p_states, s_states = [], []
    for l in range(depth):
        last = l == depth - 1
        hp, st = _prompt_layer(hp, bp, seq, w, l, last)
        p_states.append(st)
        hs, st = _sample_layer(hs, db, tq_s, w, l, last, page_table, sbk, sbv, cmpk, cmpv, selk, selv,
                               cache_win_k, cache_win_v, state_pool, state_conv)
        s_states.append(st)

    stacked = lambda states, j: jnp.stack([st[j] for st in states], axis=0)
    return (hp.reshape(bp, seq, d_model), hs.reshape(db, tq_s, d_model),
            *[stacked(p_states, j) for j in range(10)],
            *[stacked(s_states, j) for j in range(10)])


def _prepare_weights(norm_g, w_in, w_cmp_k, w_cmp_v, w_pool, pool_scale, conv_w, conv_b, ln_g, ln_b,
                     w_pw, w_out, final_g):
    depth = w_in.shape[0]
    vec3 = lambda a: a.reshape(depth, 1, -1)
    return dict(
        w_pack=_pack_w_in(w_in), w_out=w_out.astype(BF16),
        wk_exp=_expand_cmp_weight(w_cmp_k), wv_exp=_expand_cmp_weight(w_cmp_v),
        wk_t=_expand_cmp_weight_t(w_cmp_k, LANES), wv_t=_expand_cmp_weight_t(w_cmp_v, LANES),
        w_pool=w_pool.astype(BF16), w_pw=w_pw.astype(BF16),
        conv_w=jnp.pad(conv_w, ((0, 0), (0, CONV_HALO - CONV_W), (0, 0))),
        norm_g=vec3(norm_g), pool_scale=vec3(pool_scale), conv_b=vec3(conv_b),
        ln_g=vec3(ln_g), ln_b=vec3(ln_b), final_g=final_g.reshape(1, -1))


def _prompt_layer(hp, bp, seq, w, l, last):
    proj = _inproj(hp, w["norm_g"], w["w_pack"], l)
    proj3 = proj.reshape(bp, seq, N_PACK)
    col = lambda name, width: proj3[:, :, COL[name]:COL[name] + width]
    kc, vc = col("b_kc", LANES), col("b_vc", LANES)
    o_a = _sb_prompt(proj3)
    n_blocks = bp * seq // L_CMP
    kcmp, vcmp = _compress(kc.reshape(n_blocks, L_CMP * LANES), vc.reshape(n_blocks, L_CMP * LANES),
                           w["wk_exp"], w["wv_exp"], l, 0, n_blocks)
    n_cmp = seq // L_CMP
    o_cmp, sel = _nsa_cmp_prompt(proj3, kcmp.reshape(bp, n_cmp, LANES), vcmp.reshape(bp, n_cmp, LANES))
    o_b = _nsa_att_prompt(proj3, sel, o_cmp)
    zero_pool = jnp.zeros((bp, POOL_HALO, W_GRP), F32)
    zero_conv = jnp.zeros((bp, CONV_HALO, W_GRP), F32)
    o_c = _pool(proj3, COL["c_u"] // W_GRP, zero_pool, w["w_pool"], w["pool_scale"], l, 0)
    o_d, u_conv = _conv(proj3, COL["d_a"] // W_GRP, COL["d_gl"] // W_GRP, zero_conv,
                        w["conv_w"], w["conv_b"], w["ln_g"], w["ln_b"], w["w_pw"], l)
    flat = lambda a: a.reshape(bp * seq, W_GRP)
    hp_new = _outproj([flat(o_a), flat(o_b), flat(o_c), flat(o_d)], proj, hp, w["w_out"], l,
                      w["final_g"] if last else None)
    keep = min(WINDOW, seq)
    heads = lambda a, n: a.reshape(bp, -1, n, HEAD_DIM)
    states = (
        heads(col("a_k", W_GRP), 8), heads(col("a_v", W_GRP), 8),
        heads(kc, 2), heads(vc, 2),
        heads(col("b_ks", LANES), 2), heads(col("b_vs", LANES), 2),
        heads(col("b_kw", LANES)[:, seq - keep:], 2), heads(col("b_vw", LANES)[:, seq - keep:], 2),
        col("c_u", W_GRP)[:, seq - (POOL_HALO - 1):], u_conv[:, seq - (CONV_W - 1):])
    return hp_new, states


def _sample_layer(hs, db, tq_s, w, l, last, page_table, sbk, sbv, cmpk, cmpv, selk, selv,
                  cache_win_k, cache_win_v, state_pool, state_conv):
    depth, n_pool = sbk.shape[:2]
    page = sbk.shape[4]
    n_pages = page_table.shape[1]
    past = n_pages * page
    projs = _inproj(hs, w["norm_g"], w["w_pack"], l)
    projs3 = projs.reshape(db, tq_s, N_PACK)
    scol = lambda name, width: projs3[:, :, COL[name]:COL[name] + width]
    pad8 = lambda a: jnp.pad(a, ((0, 0), (0, 8 - tq_s), (0, 0)))
    head_of_col = jnp.arange(W_GRP) // HEAD_DIM
    sb_mask = (head_of_col[None, :] == jnp.arange(8)[:, None]).astype(F32)
    a_q = scol("a_q", W_GRP) * SCALE
    qbd = (a_q[:, :, None, :] * sb_mask[None, None]).reshape(db, tq_s * 8, W_GRP)
    o_a = _sb_sample(page_table, qbd, pad8(scol("a_k", W_GRP)), pad8(scol("a_v", W_GRP)), sbk, sbv, l)
    o_a = o_a.reshape(db * tq_s, W_GRP)
    kall = _compress_cache(cmpk, w["wk_t"], l, n_pool * 2)
    vall = _compress_cache(cmpv, w["wv_t"], l, n_pool * 2)
    per_page = page // L_CMP
    relane = lambda a: a.reshape(n_pool, 2, per_page, HEAD_DIM).transpose(0, 2, 1, 3).reshape(
        n_pool, per_page * LANES)
    kall, vall = relane(kall), relane(vall)
    b_q = _sample_rows(scol("b_q", W_GRP) * SCALE, HEAD_DIM)
    kvh_of_row = jnp.arange(b_q.shape[1]) & 1
    q_rows = (b_q[:, :, None, :] * jnp.eye(2, dtype=F32)[kvh_of_row][None, :, :, None]).reshape(db, -1, LANES)
    br_rows = jnp.pad(_sample_rows(scol("b_br", 24), 3), ((0, 0), (0, 0), (0, LANES - 3)))
    kw_new, vw_new = scol("b_kw", LANES), scol("b_vw", LANES)
    wink = jnp.transpose(cache_win_k, (0, 1, 3, 4, 2))
    winv = jnp.transpose(cache_win_v, (0, 1, 3, 4, 2))
    o_cmp, o_win, sel, used = _nsa_small(page_table, q_rows, kall, vall, wink, winv,
                                         pad8(kw_new), pad8(vw_new), l, past)
    o_b_rows = _nsa_sel_sample(page_table, used[:, 0], q_rows, sel, pad8(scol("b_ks", LANES)),
                               pad8(scol("b_vs", LANES)), o_cmp, o_win, br_rows, selk, selv, l, past)
    o_b = o_b_rows.reshape(db, 4, tq_s, 2, HEAD_DIM).transpose(0, 2, 3, 1, 4).reshape(db * tq_s, W_GRP)
    pool_pre = jnp.pad(state_pool[l], ((0, 0), (POOL_HALO - state_pool.shape[2], 0), (0, 0)))
    o_c = _pool(pad8(scol("c_u", W_GRP)), 0, pool_pre, w["w_pool"], w["pool_scale"], l, past)
    o_c = o_c[:, :tq_s].reshape(db * tq_s, W_GRP)
    conv_pre = jnp.pad(state_conv[l], ((0, 0), (CONV_HALO - state_conv.shape[2], 0), (0, 0)))
    o_d, u_conv = _conv(pad8(scol("d_a", 2 * W_GRP)), 0, 1, conv_pre,
                        w["conv_w"], w["conv_b"], w["ln_g"], w["ln_b"], w["w_pw"], l)
    o_d = o_d[:, :tq_s].reshape(db * tq_s, W_GRP)
    hs_new = _outproj([o_a, o_b, o_c, o_d], projs, hs, w["w_out"], l, w["final_g"] if last else None)
    sheads = lambda a, n: a.reshape(db, -1, n, HEAD_DIM)
    states = (
        sheads(scol("a_k", W_GRP), 8), sheads(scol("a_v", W_GRP), 8),
        sheads(scol("b_kc", LANES), 2), sheads(scol("b_vc", LANES), 2),
        sheads(scol("b_ks", LANES), 2), sheads(scol("b_vs", LANES), 2),
        jnp.concatenate([cache_win_k[l][:, tq_s:], sheads(kw_new, 2)], axis=1),
        jnp.concatenate([cache_win_v[l][:, tq_s:], sheads(vw_new, 2)], axis=1),
        jnp.concatenate([state_pool[l][:, tq_s:], scol("c_u", W_GRP)], axis=1),
        jnp.concatenate([state_conv[l][:, tq_s:], u_conv[:, :tq_s]], axis=1))
    return hs_new, states
```

```python
import functools

import jax
import jax.numpy as jnp
from jax import lax
from jax.experimental import pallas as pl
from jax.experimental.pallas import tpu as pltpu

F32 = jnp.float32
BF16 = jnp.bfloat16

HEAD_DIM = 64
LANES = 128
SUBLANES = 8
W_GRP = 512
L_CMP = 32
L_SEL = 64
N_SEL = 16
WINDOW = 512
FORCE_BONUS = 1000.0
POOL_WINDOWS = (2, 4, 8, 16)
POOL_HALO = 16
CONV_W = 31
CONV_HALO = 32
RMS_EPS = 1e-6
LN_EPS = 1e-5
NEG_BIG = -1e30
SCALE = HEAD_DIM ** -0.5
SB_DEAD_CARRY = -104.0

COL = dict(a_q=0, a_k=512, a_v=1024, a_g=1536, b_q=2048, b_g=2560, c_u=3072, c_g=3584,
           d_a=4096, d_gl=4608, d_g=5120, b_kc=5632, b_vc=5760, b_ks=5888, b_vs=6016,
           b_kw=6144, b_vw=6272, b_br=6400)
N_PACK = 6656
_SRC = (("a_q", 512), ("a_k", 512), ("a_v", 512), ("a_g", 512), ("b_q", 512), ("b_kc", 128),
        ("b_vc", 128), ("b_ks", 128), ("b_vs", 128), ("b_kw", 128), ("b_vw", 128), ("b_g", 512),
        ("b_br", 24), ("c_u", 512), ("c_g", 512), ("d_a", 512), ("d_gl", 512), ("d_g", 512))


def _params(sem, vmem_mb=None):
    kw = dict(dimension_semantics=sem)
    if vmem_mb is not None:
        kw["vmem_limit_bytes"] = vmem_mb * 2 ** 20
    return pltpu.CompilerParams(**kw)


def _sigmoid(x):
    return 1.0 / (1.0 + jnp.exp(-x))


def _iota(shape, axis):
    return lax.broadcasted_iota(jnp.int32, shape, axis)


def _dot_nt(a, b):
    return lax.dot_general(a, b, (((1,), (1,)), ((), ())), preferred_element_type=F32)


def _dot(a, b):
    return jnp.dot(a, b, preferred_element_type=F32)


def _split_dot(x, m):
    hi = x.astype(BF16)
    lo = (x - hi.astype(F32)).astype(BF16)
    return _dot(hi, m) + _dot(lo, m)


def _log_keep(z):
    return jnp.minimum(-z, 0.0) - jnp.log(1.0 + jnp.exp(-jnp.abs(z)))


def _upper_ones(n):
    return jnp.where(_iota((n, n), 0) > _iota((n, n), 1), 1.0, 0.0).astype(BF16)


def _inproj_kernel(x_ref, g_ref, w_ref, o_ref, hn_ref):
    @pl.when(pl.program_id(1) == 0)
    def _():
        x = x_ref[...]
        ms = jnp.mean(x * x, axis=-1, keepdims=True)
        hn_ref[...] = ((x * lax.rsqrt(ms + RMS_EPS)) * g_ref[...]).astype(BF16)

    o_ref[...] = _dot_nt(hn_ref[...], w_ref[...])


def _inproj(x, norm_g, w_pack, layer):
    rows, d = x.shape
    n = w_pack.shape[1]
    tm = min(rows, 1024)
    tn = n // 4 if (n // 4) % LANES == 0 else 512
    return pl.pallas_call(
        _inproj_kernel,
        grid=(rows // tm, n // tn),
        in_specs=[pl.BlockSpec((tm, d), lambda i, j: (i, 0)),
                  pl.BlockSpec((None, 1, d), lambda i, j: (layer, 0, 0)),
                  pl.BlockSpec((None, tn, d), lambda i, j: (layer, j, 0))],
        out_specs=pl.BlockSpec((tm, tn), lambda i, j: (i, j)),
        out_shape=jax.ShapeDtypeStruct((rows, n), F32),
        scratch_shapes=[pltpu.VMEM((tm, d), BF16)],
        compiler_params=_params(("parallel", "arbitrary"), 56),
        name="inproj",
    )(x, norm_g, w_pack)


def _outproj_kernel(oa, ob, oc, od, ga, gb, gc, gd, h_ref, w_ref, *rest, final):
    if final:
        fg_ref, o_ref = rest
    else:
        (o_ref,) = rest
    acc = h_ref[...]
    for i, (o, g) in enumerate(((oa, ga), (ob, gb), (oc, gc), (od, gd))):
        gv = g[...]
        mix = (o[...] * (gv * _sigmoid(gv))).astype(BF16)
        acc = acc + _dot(mix, w_ref[i * W_GRP:(i + 1) * W_GRP, :])
    if final:
        ms = jnp.mean(acc * acc, axis=-1, keepdims=True)
        acc = (acc * lax.rsqrt(ms + RMS_EPS)) * fg_ref[...]
    o_ref[...] = acc


def _outproj(outs, proj, h, w_out, layer, final_g=None):
    rows, d = h.shape
    tm = min(rows, 256)
    final = final_g is not None
    gate_blocks = [COL[k] // W_GRP for k in ("a_g", "b_g", "c_g", "d_g")]
    in_specs = [pl.BlockSpec((tm, W_GRP), lambda i: (i, 0)) for _ in range(4)]
    in_specs += [pl.BlockSpec((tm, W_GRP), functools.partial(lambda i, cb: (i, cb), cb=cb))
                 for cb in gate_blocks]
    in_specs += [pl.BlockSpec((tm, d), lambda i: (i, 0)),
                 pl.BlockSpec((None, 4 * W_GRP, d), lambda i: (layer, 0, 0))]
    args = list(outs) + [proj] * 4 + [h, w_out]
    if final:
        in_specs.append(pl.BlockSpec((1, d), lambda i: (0, 0)))
        args.append(final_g)
    return pl.pallas_call(
        functools.partial(_outproj_kernel, final=final),
        grid=(rows // tm,),
        in_specs=in_specs,
        out_specs=pl.BlockSpec((tm, d), lambda i: (i, 0)),
        out_shape=jax.ShapeDtypeStruct((rows, d), F32),
        compiler_params=_params(("parallel",), 48),
        name="outproj",
    )(*args)


SB_PAIRS = 2
SB_TILE = 256


def _sb_prompt_kernel(q_ref, k_ref, v_ref, o_ref, *, tq):
    qt = pl.program_id(2)
    lane = _iota((tq, LANES), 1)
    row = _iota((tq, tq), 0)
    col = _iota((tq, tq), 1)
    before = col < row
    upper = _upper_ones(tq)
    q = q_ref[0] * SCALE
    chains = []
    for pair in range(SB_PAIRS):
        qp = q[:, pair * LANES:(pair + 1) * LANES]
        for hh in range(2):
            in_head = (lane >= hh * HEAD_DIM) & (lane < (hh + 1) * HEAD_DIM)
            chains.append((pair, jnp.where(in_head, qp, 0.0).astype(BF16)))

    def tile(kt, state, diag):
        start = pl.multiple_of(kt * tq, tq)
        kv = [(k_ref[0, pl.ds(start, tq), p * LANES:(p + 1) * LANES].astype(BF16),
               v_ref[0, pl.ds(start, tq), p * LANES:(p + 1) * LANES].astype(BF16))
              for p in range(SB_PAIRS)]
        new = []
        for (pair, qm), (carry, acc) in zip(chains, state):
            k, v = kv[pair]
            z = _dot_nt(qm, k)
            lk = _log_keep(z)
            if diag:
                lk = jnp.where(before, lk, 0.0)
            between = _dot(lk.astype(BF16), upper) + carry
            a = jnp.exp(z + lk + between)
            if diag:
                a = jnp.where(before, a, 0.0)
            new.append((carry + jnp.sum(lk, axis=1, keepdims=True), acc + _dot(a.astype(BF16), v)))
        return tuple(new)

    def any_live(state):
        top = functools.reduce(jnp.maximum, [jnp.max(carry) for carry, _ in state])
        return top > SB_DEAD_CARRY

    zero = (jnp.zeros((tq, 1), F32), jnp.zeros((tq, LANES), F32))
    state = tile(qt, (zero,) * len(chains), True)

    def cond(c):
        return (c[0] < qt) & c[1]

    def body(c):
        st = tile(qt - 1 - c[0], c[2], False)
        return c[0] + 1, any_live(st), st

    _, _, state = lax.while_loop(cond, body, (jnp.int32(0), any_live(state), state))
    for pair in range(SB_PAIRS):
        o_ref[0, :, pair * LANES:(pair + 1) * LANES] = jnp.where(
            lane < HEAD_DIM, state[2 * pair][1], state[2 * pair + 1][1])


def _sb_prompt(proj3):
    b, t, _ = proj3.shape
    tq = min(t, SB_TILE)
    width = SB_PAIRS * LANES
    qb, kb, vb = (COL[k] // width for k in ("a_q", "a_k", "a_v"))
    return pl.pallas_call(
        functools.partial(_sb_prompt_kernel, tq=tq),
        grid=(b, W_GRP // width, t // tq),
        in_specs=[pl.BlockSpec((1, tq, width), lambda bi, p, i: (bi, i, qb + p)),
                  pl.BlockSpec((1, t, width), lambda bi, p, i: (bi, 0, kb + p)),
                  pl.BlockSpec((1, t, width), lambda bi, p, i: (bi, 0, vb + p))],
        out_specs=pl.BlockSpec((1, tq, width), lambda bi, p, i: (bi, i, p)),
        out_shape=jax.ShapeDtypeStruct((b, t, W_GRP), F32),
        compiler_params=_params(("parallel", "parallel", "arbitrary")),
        name="sb_prompt",
    )(proj3, proj3, proj3)


def _cmp_kernel(xk_ref, xv_ref, wk_ref, wv_ref, ok_ref, ov_ref):
    ok_ref[...] = _dot(xk_ref[...].astype(BF16), wk_ref[...])
    ov_ref[...] = _dot(xv_ref[...].astype(BF16), wv_ref[...])


def _compress(xk, xv, wk_exp, wv_exp, layer, row0, rows):
    kdim = xk.shape[1]
    tr = min(rows, 256)
    off = row0 // tr
    return pl.pallas_call(
        _cmp_kernel,
        grid=(rows // tr,),
        in_specs=[pl.BlockSpec((tr, kdim), lambda i: (off + i, 0)),
                  pl.BlockSpec((tr, kdim), lambda i: (off + i, 0)),
                  pl.BlockSpec((None, kdim, LANES), lambda i: (layer, 0, 0)),
                  pl.BlockSpec((None, kdim, LANES), lambda i: (layer, 0, 0))],
        out_specs=[pl.BlockSpec((tr, LANES), lambda i: (i, 0))] * 2,
        out_shape=[jax.ShapeDtypeStruct((rows, LANES), F32)] * 2,
        compiler_params=_params(("parallel",), 48),
        name="compress",
    )(xk, xv, wk_exp, wv_exp)


def _cmp_cache_kernel(x_ref, w_ref, o_ref, *, rows):
    acc = jnp.zeros((rows, o_ref.shape[1]), F32)
    for d in range(HEAD_DIM):
        acc = acc + _dot(x_ref[pl.ds(d, rows, stride=HEAD_DIM), :].astype(BF16), w_ref[d])
    o_ref[...] = acc


CMP_CACHE_ROWS = 512


def _compress_cache(x, w_t, layer, n_rows):
    page = x.shape[1]
    n_out = w_t.shape[3]
    tr = CMP_CACHE_ROWS if n_rows % CMP_CACHE_ROWS == 0 else n_rows
    off = layer * (n_rows // tr)
    return pl.pallas_call(
        functools.partial(_cmp_cache_kernel, rows=tr),
        grid=(n_rows // tr,),
        in_specs=[pl.BlockSpec((tr * HEAD_DIM, page), lambda i: (off + i, 0)),
                  pl.BlockSpec((None, HEAD_DIM, page, n_out), lambda i: (layer, 0, 0, 0))],
        out_specs=pl.BlockSpec((tr, n_out), lambda i: (i, 0)),
        out_shape=jax.ShapeDtypeStruct((n_rows, n_out), F32),
        compiler_params=_params(("parallel",), 56),
        name="compress_cache",
    )(x, w_t)


def _head_slope(h):
    return 2.0 ** -(h + 1)


def _half_variant(x, x_swapped, kvh, par, low):
    src = x if kvh == par else x_swapped
    keep = low if par == 0 else jnp.logical_not(low)
    return jnp.where(keep, src, 0.0).astype(BF16)


def _nsa_cmp_kernel(q_ref, kc_ref, vc_ref, oc_ref, sel_ref, *, tq, ncmp):
    qt = pl.program_id(1)
    nsel = ncmp // 2
    ntop = min(N_SEL, nsel)
    half = LANES // 2

    def permuted(ref):
        ev = ref[0, pl.ds(0, nsel, stride=2), :]
        od = ref[0, pl.ds(1, nsel, stride=2), :]
        if nsel == half:
            return jnp.concatenate([ev, od], axis=0)
        pad = jnp.zeros((half - nsel, LANES), F32)
        return jnp.concatenate([ev, pad, od, pad], axis=0)

    kc = permuted(kc_ref)
    vc = permuted(vc_ref)
    kc_sw = pltpu.roll(kc, half, 1)
    vc_sw = pltpu.roll(vc, half, 1)
    low = _iota((LANES, LANES), 1) < half

    n_i = _iota((tq, LANES), 1)
    odd = n_i >= half
    slot = jnp.where(odd, n_i - half, n_i)
    cmp_blk = jnp.where(odd, 2 * slot + 1, 2 * slot)
    qpos = qt * tq + _iota((tq, LANES), 0)
    dist = (qpos - (cmp_blk * L_CMP + (L_CMP - 1))).astype(F32)
    ok = (dist >= 0) & (slot < nsel)

    q = q_ref[0] * SCALE
    o_pairs = [jnp.zeros((tq, LANES), F32) for _ in range(4)]
    for kvh in range(2):
        imp = jnp.zeros((tq, LANES), F32)
        for g in range(4):
            h = kvh * 4 + g
            par, pair = h % 2, h // 2
            qb = q[:, pair * LANES:(pair + 1) * LANES].astype(BF16)
            z = _dot_nt(qb, _half_variant(kc, kc_sw, kvh, par, low))
            s = jnp.where(ok, z - _head_slope(h) * dist, NEG_BIG)
            e = jnp.exp(s - jnp.max(s, axis=1, keepdims=True))
            p = jnp.where(ok, e / jnp.sum(e, axis=1, keepdims=True), 0.0)
            imp = imp + p
            o_pairs[pair] = o_pairs[pair] + _dot(p.astype(BF16), _half_variant(vc, vc_sw, kvh, par, low))
        imp_sel = imp + pltpu.roll(imp, half, 1)
        blk = n_i
        cur = qpos >> 6
        forced = (blk == 0) | (blk == cur) | (blk == cur - 1)
        valid = (blk * L_SEL <= qpos) & (blk < nsel)
        score = jnp.where(valid, imp_sel + jnp.where(forced, FORCE_BONUS, 0.0), -jnp.inf)
        def ranked(score=score, valid=valid):
            rank = jnp.zeros((tq, LANES), F32)
            for i in range(nsel):
                si = score[:, i:i + 1]
                beats = (si > score) | ((si == score) & (blk > i))
                rank = rank + jnp.where(beats, 1.0, 0.0)
            return jnp.where(valid & (rank < ntop), 1.0, 0.0)

        sel_ref[0, kvh] = lax.cond((qt + 1) * tq > ntop * L_SEL, ranked,
                                   lambda valid=valid: jnp.where(valid, 1.0, 0.0))
    oc_ref[0] = jnp.concatenate(o_pairs, axis=1)


def _nsa_cmp_prompt(proj3, kcmp, vcmp):
    b, t, _ = proj3.shape
    ncmp = kcmp.shape[1]
    tq = min(t, 256)
    qb = COL["b_q"] // W_GRP
    return pl.pallas_call(
        functools.partial(_nsa_cmp_kernel, tq=tq, ncmp=ncmp),
        grid=(b, t // tq),
        in_specs=[pl.BlockSpec((1, tq, W_GRP), lambda bi, i: (bi, i, qb)),
                  pl.BlockSpec((1, ncmp, LANES), lambda bi, i: (bi, 0, 0)),
                  pl.BlockSpec((1, ncmp, LANES), lambda bi, i: (bi, 0, 0))],
        out_specs=[pl.BlockSpec((1, tq, W_GRP), lambda bi, i: (bi, i, 0)),
                   pl.BlockSpec((1, 2, tq, LANES), lambda bi, i: (bi, 0, i, 0))],
        out_shape=[jax.ShapeDtypeStruct((b, t, W_GRP), F32),
                   jax.ShapeDtypeStruct((b, 2, t, LANES), F32)],
        compiler_params=_params(("parallel", "parallel")),
        name="nsa_cmp_prompt",
    )(proj3, kcmp, vcmp)


def _nsa_att_kernel(q_ref, ks_ref, vs_ref, kw_ref, vw_ref, sel_ref, br_ref, oc_ref, o_ref,
                    ksv, vsv, kwv, vwv, *, tq, t):
    qt = pl.program_id(1)

    @pl.when(qt == 0)
    def _():
        low = _iota((t, LANES), 1) < LANES // 2
        for src, dst in ((ks_ref, ksv), (vs_ref, vsv), (kw_ref, kwv), (vw_ref, vwv)):
            x = src[0]
            xs = pltpu.roll(x, LANES // 2, 1)
            for kvh in range(2):
                for par in range(2):
                    dst[kvh * 2 + par] = _half_variant(x, xs, kvh, par, low)

    q = q_ref[0] * SCALE
    sig = _sigmoid(br_ref[0])
    oc = oc_ref[0]
    rel = (_iota((tq, tq), 0) - _iota((tq, tq), 1)).astype(F32)
    lane = _iota((tq, LANES), 1)
    blk_row = _iota((LANES, tq), 0)
    blk_col = _iota((LANES, tq), 1)
    init1 = (jnp.full((tq, 1), NEG_BIG, F32), jnp.zeros((tq, 1), F32), jnp.zeros((tq, LANES), F32))
    n_win = jnp.minimum(qt, (WINDOW + tq - 1) // tq) + 1

    for kvh in range(2):
        heads = [kvh * 4 + g for g in range(4)]
        qbs = [q[:, (h // 2) * LANES:(h // 2 + 1) * LANES].astype(BF16) for h in heads]
        selk = sel_ref[0, kvh].astype(BF16)

        def tile(kt, state, kv, vv, selected, kvh=kvh, heads=heads, qbs=qbs, selk=selk):
            start = pl.multiple_of(kt * tq, tq)
            dist = rel + ((qt - kt) * tq).astype(F32)
            if selected:
                expand = jnp.where(blk_row == ((start + blk_col) >> 6), 1.0, 0.0).astype(BF16)
                mask = (_dot(selk, expand) > 0.5) & (dist >= 0)
            else:
                mask = (dist >= 0) & (dist < WINDOW)
            bias = jnp.where(mask, 0.0, NEG_BIG)
            kvs = [(kv[kvh * 2 + par, pl.ds(start, tq), :], vv[kvh * 2 + par, pl.ds(start, tq), :])
                   for par in range(2)]
            new = []
            for h, qb, (m, l, acc) in zip(heads, qbs, state):
                k, v = kvs[h % 2]
                s = (_dot_nt(qb, k) - _head_slope(h) * dist) + bias
                m_new = jnp.maximum(m, jnp.max(s, axis=1, keepdims=True))
                p = jnp.exp(s - m_new)
                alpha = jnp.exp(m - m_new)
                new.append((m_new, alpha * l + jnp.sum(p, axis=1, keepdims=True),
                            alpha * acc + _dot(p.astype(BF16), v)))
            return tuple(new)

        init = (init1,) * len(heads)
        st_s = lax.fori_loop(0, qt + 1, lambda i, st: tile(qt - i, st, ksv, vsv, True), init)
        st_w = lax.fori_loop(0, n_win, lambda i, st: tile(qt - i, st, kwv, vwv, False), init)
        o_heads = []
        for i, h in enumerate(heads):
            g_c = sig[:, 3 * h:3 * h + 1]
            g_s = sig[:, 3 * h + 1:3 * h + 2]
            g_w = sig[:, 3 * h + 2:3 * h + 3]
            oc_pair = oc[:, (h // 2) * LANES:(h // 2 + 1) * LANES]
            o_heads.append(g_c * oc_pair + g_s * (st_s[i][2] / st_s[i][1])
                           + g_w * (st_w[i][2] / st_w[i][1]))
        for j in range(2):
            pair = kvh * 2 + j
            o_ref[0, :, pair * LANES:(pair + 1) * LANES] = jnp.where(
                lane < HEAD_DIM, o_heads[2 * j], o_heads[2 * j + 1])


def _nsa_att_prompt(proj3, sel, oc):
    b, t, _ = proj3.shape
    tq = min(t, 256)
    qb = COL["b_q"] // W_GRP
    kv_blocks = [COL[k] // LANES for k in ("b_ks", "b_vs", "b_kw", "b_vw")]
    br = COL["b_br"] // LANES
    in_specs = [pl.BlockSpec((1, tq, W_GRP), lambda bi, i: (bi, i, qb))]
    in_specs += [pl.BlockSpec((1, t, LANES), functools.partial(lambda bi, i, cb: (bi, 0, cb), cb=cb))
                 for cb in kv_blocks]
    in_specs += [pl.BlockSpec((1, 2, tq, LANES), lambda bi, i: (bi, 0, i, 0)),
                 pl.BlockSpec((1, tq, LANES), lambda bi, i: (bi, i, br)),
                 pl.BlockSpec((1, tq, W_GRP), lambda bi, i: (bi, i, 0))]
    return pl.pallas_call(
        functools.partial(_nsa_att_kernel, tq=tq, t=t),
        grid=(b, t // tq),
        in_specs=in_specs,
        out_specs=pl.BlockSpec((1, tq, W_GRP), lambda bi, i: (bi, i, 0)),
        out_shape=jax.ShapeDtypeStruct((b, t, W_GRP), F32),
        scratch_shapes=[pltpu.VMEM((4, t, LANES), BF16)] * 4,
        compiler_params=_params(("parallel", "arbitrary"), 48),
        name="nsa_att_prompt",
    )(proj3, proj3, proj3, proj3, proj3, sel, proj3, oc)


def _pool_kernel(*refs, tt, start, has_halo):
    if has_halo:
        u_ref, halo_ref, pre_ref, w_ref, sc_ref, o_ref, e_ref, d_ref = refs
    else:
        u_ref, pre_ref, w_ref, sc_ref, o_ref, e_ref, d_ref = refs
    ti = pl.program_id(1)
    e_ref[POOL_HALO:POOL_HALO + tt, :] = u_ref[0]
    if has_halo:
        @pl.when(ti == 0)
        def _():
            e_ref[0:POOL_HALO, :] = pre_ref[0]

        @pl.when(ti > 0)
        def _():
            e_ref[0:POOL_HALO, :] = halo_ref[0]
    else:
        e_ref[0:POOL_HALO, :] = pre_ref[0]

    ch = min(tt, 64)
    for c in range(tt // ch):
        r0 = c * ch
        pos = start + ti * tt + r0 + _iota((ch, LANES), 0)
        for g, win in enumerate(POOL_WINDOWS):
            lanes = slice(g * LANES, (g + 1) * LANES)
            cur = e_ref[POOL_HALO + r0:POOL_HALO + r0 + ch, lanes]
            total = cur
            for s in range(1, win):
                total = total + e_ref[POOL_HALO + r0 - s:POOL_HALO + r0 - s + ch, lanes]
            cnt = jnp.minimum(win, pos + 1).astype(F32)
            d_ref[r0:r0 + ch, lanes] = total / cnt - cur
    for g in range(len(POOL_WINDOWS)):
        lanes = slice(g * LANES, (g + 1) * LANES)
        y = _dot(d_ref[:, lanes].astype(BF16), w_ref[g])
        o_ref[0, :, lanes] = y * sc_ref[:, lanes]


def _pool(u_arr, cb, prefix, w_pool, pool_scale, layer, start):
    b, t, _ = u_arr.shape
    tt = min(t, 512)
    nt = t // tt
    has_halo = nt > 1
    in_specs = [pl.BlockSpec((1, tt, W_GRP), lambda bi, i: (bi, i, cb))]
    args = [u_arr]
    if has_halo:
        per = tt // POOL_HALO
        in_specs.append(pl.BlockSpec((1, POOL_HALO, W_GRP),
                                     lambda bi, i: (bi, jnp.maximum(i * per - 1, 0), cb)))
        args.append(u_arr)
    in_specs += [pl.BlockSpec((1, POOL_HALO, W_GRP), lambda bi, i: (bi, 0, 0)),
                 pl.BlockSpec((None, 4, LANES, LANES), lambda bi, i: (layer, 0, 0, 0)),
                 pl.BlockSpec((None, 1, W_GRP), lambda bi, i: (layer, 0, 0))]
    args += [prefix, w_pool, pool_scale]
    return pl.pallas_call(
        functools.partial(_pool_kernel, tt=tt, start=start, has_halo=has_halo),
        grid=(b, nt),
        in_specs=in_specs,
        out_specs=pl.BlockSpec((1, tt, W_GRP), lambda bi, i: (bi, i, 0)),
        out_shape=jax.ShapeDtypeStruct((b, t, W_GRP), F32),
        scratch_shapes=[pltpu.VMEM((POOL_HALO + tt, W_GRP), F32), pltpu.VMEM((tt, W_GRP), F32)],
        compiler_params=_params(("parallel", "arbitrary")),
        name="pool",
    )(*args)


def _conv_kernel(*refs, tt, has_halo):
    if has_halo:
        (a_ref, g_ref, ha_ref, hg_ref, pre_ref, cw_ref, cb_ref, lg_ref, lb_ref, pw_ref,
         o_ref, u_ref, e_ref, y_ref) = refs
    else:
        (a_ref, g_ref, pre_ref, cw_ref, cb_ref, lg_ref, lb_ref, pw_ref,
         o_ref, u_ref, e_ref, y_ref) = refs
    ti = pl.program_id(1)
    u = a_ref[0] * _sigmoid(g_ref[0])
    u_ref[0] = u
    e_ref[0, CONV_HALO:CONV_HALO + tt, :] = u
    if has_halo:
        @pl.when(ti == 0)
        def _():
            e_ref[0, 0:CONV_HALO, :] = pre_ref[0]

        @pl.when(ti > 0)
        def _():
            e_ref[0, 0:CONV_HALO, :] = ha_ref[0] * _sigmoid(hg_ref[0])
    else:
        e_ref[0, 0:CONV_HALO, :] = pre_ref[0]

    n_shift = CONV_HALO + tt - SUBLANES
    for s in range(1, SUBLANES):
        for r in range(0, n_shift, 64):
            rows = min(64, n_shift - r)
            e_ref[s, r:r + rows, :] = e_ref[0, r + s:r + s + rows, :]

    ch = min(tt, 32)
    lead = CONV_HALO - (CONV_W - 1)
    for c in range(tt // ch):
        r0 = c * ch
        acc = jnp.broadcast_to(cb_ref[...], (ch, W_GRP))
        for j in range(CONV_W):
            s = (lead + j) % SUBLANES
            base = r0 + lead + j - s
            acc = acc + cw_ref[j:j + 1, :] * e_ref[s, base:base + ch, :]
        y_ref[r0:r0 + ch, :] = acc
    y = y_ref[...]
    yc = y - jnp.mean(y, axis=-1, keepdims=True)
    yn = yc * lax.rsqrt(jnp.mean(yc * yc, axis=-1, keepdims=True) + LN_EPS)
    yn = yn * lg_ref[...] + lb_ref[...]
    o_ref[0] = _dot((yn * _sigmoid(yn)).astype(BF16), pw_ref[...])


def _conv(arr, cb_a, cb_g, prefix, conv_w, conv_b, ln_g, ln_b, w_pw, layer):
    b, t, _ = arr.shape
    tt = min(t, 512)
    nt = t // tt
    has_halo = nt > 1
    in_specs = [pl.BlockSpec((1, tt, W_GRP), lambda bi, i: (bi, i, cb_a)),
                pl.BlockSpec((1, tt, W_GRP), lambda bi, i: (bi, i, cb_g))]
    args = [arr, arr]
    if has_halo:
        per = tt // CONV_HALO
        in_specs += [pl.BlockSpec((1, CONV_HALO, W_GRP),
                                  lambda bi, i: (bi, jnp.maximum(i * per - 1, 0), cb_a)),
                     pl.BlockSpec((1, CONV_HALO, W_GRP),
                                  lambda bi, i: (bi, jnp.maximum(i * per - 1, 0), cb_g))]
        args += [arr, arr]
    vec = lambda: pl.BlockSpec((None, 1, W_GRP), lambda bi, i: (layer, 0, 0))
    in_specs += [pl.BlockSpec((1, CONV_HALO, W_GRP), lambda bi, i: (bi, 0, 0)),
                 pl.BlockSpec((None, CONV_HALO, W_GRP), lambda bi, i: (layer, 0, 0)),
                 vec(), vec(), vec(),
                 pl.BlockSpec((None, W_GRP, W_GRP), lambda bi, i: (layer, 0, 0))]
    args += [prefix, conv_w, conv_b, ln_g, ln_b, w_pw]
    return pl.pallas_call(
        functools.partial(_conv_kernel, tt=tt, has_halo=has_halo),
        grid=(b, nt),
        in_specs=in_specs,
        out_specs=[pl.BlockSpec((1, tt, W_GRP), lambda bi, i: (bi, i, 0))] * 2,
        out_shape=[jax.ShapeDtypeStruct((b, t, W_GRP), F32)] * 2,
        scratch_shapes=[pltpu.VMEM((SUBLANES, CONV_HALO + tt, W_GRP), F32),
                        pltpu.VMEM((tt, W_GRP), F32)],
        compiler_params=_params(("parallel", "arbitrary"), 48),
        name="conv",
    )(*args)


SB_ROWS = 32
PAGES_PER_STEP = 16


def _sb_sample_kernel(pt_ref, q_ref, kn_ref, vn_ref, k_hbm, v_hbm, o_ref,
                      kbuf, vbuf, sem, carry_ref, acc_ref, *, layer, n_pages, page):
    b = pl.program_id(0)
    qbd = q_ref[0].astype(BF16)
    upper = _upper_ones(page)

    def page_copies(i, slot):
        pg = pt_ref[b * n_pages + (n_pages - 1 - i)]
        return (pltpu.make_async_copy(k_hbm.at[layer, pg], kbuf.at[slot], sem.at[0, slot]),
                pltpu.make_async_copy(v_hbm.at[layer, pg], vbuf.at[slot], sem.at[1, slot]))

    def start(i, slot):
        for c in page_copies(i, slot):
            c.start()

    def wait(i, slot):
        for c in page_copies(i, slot):
            c.wait()

    start(0, 0)

    def absorb(k, v, mask, channel_major):
        kb, vb = k.astype(BF16), v.astype(BF16)
        z = _dot(qbd, kb) if channel_major else _dot_nt(qbd, kb)
        lk = _log_keep(z)
        if mask is not None:
            lk = jnp.where(mask, lk, 0.0)
        between = _split_dot(lk, upper) + carry_ref[...]
        a = jnp.exp(z + lk + between)
        if mask is not None:
            a = jnp.where(mask, a, 0.0)
        ab = a.astype(BF16)
        acc_ref[...] += _dot_nt(ab, vb) if channel_major else _dot(ab, vb)
        carry_ref[...] += jnp.sum(lk, axis=1, keepdims=True)

    carry_ref[...] = jnp.zeros_like(carry_ref)
    acc_ref[...] = jnp.zeros_like(acc_ref)
    pad = jnp.zeros((page - kn_ref.shape[1], W_GRP), F32)
    r = _iota((SB_ROWS, page), 0)
    j = _iota((SB_ROWS, page), 1)
    absorb(jnp.concatenate([kn_ref[0], pad], axis=0),
           jnp.concatenate([vn_ref[0], pad], axis=0), j < (r >> 3), False)

    def live():
        return jnp.max(carry_ref[...]) > SB_DEAD_CARRY

    def body(c):
        i = c[0]
        slot = i & 1
        wait(i, slot)

        @pl.when(i + 1 < n_pages)
        def _():
            start(i + 1, 1 - slot)

        absorb(kbuf[slot].reshape(W_GRP, page), vbuf[slot].reshape(W_GRP, page), None, True)
        return i + 1, live()

    i_end, _ = lax.while_loop(lambda c: (c[0] < n_pages) & c[1], body, (jnp.int32(0), live()))

    @pl.when(i_end < n_pages)
    def _():
        wait(i_end, i_end & 1)

    acc = acc_ref[...]
    r = _iota((SB_ROWS, HEAD_DIM), 0)
    out = jnp.zeros((SB_ROWS, HEAD_DIM), F32)
    for h in range(W_GRP // HEAD_DIM):
        out = jnp.where((r & 7) == h, acc[:, h * HEAD_DIM:(h + 1) * HEAD_DIM], out)
    o_ref[0] = out


def _sb_sample(page_table, qbd, k_new, v_new, cache_k, cache_v, layer):
    db, n_pages = page_table.shape
    n_heads, page = cache_k.shape[2], cache_k.shape[4]
    small = lambda r: pl.BlockSpec((1, r, W_GRP), lambda bi, pt: (bi, 0, 0))
    grid_spec = pltpu.PrefetchScalarGridSpec(
        num_scalar_prefetch=1,
        grid=(db,),
        in_specs=[small(SB_ROWS), small(k_new.shape[1]), small(v_new.shape[1]),
                  pl.BlockSpec(memory_space=pl.ANY), pl.BlockSpec(memory_space=pl.ANY)],
        out_specs=pl.BlockSpec((1, SB_ROWS, HEAD_DIM), lambda bi, pt: (bi, 0, 0)),
        scratch_shapes=[pltpu.VMEM((2, n_heads, HEAD_DIM, page), F32),
                        pltpu.VMEM((2, n_heads, HEAD_DIM, page), F32),
                        pltpu.SemaphoreType.DMA((2, 2)),
                        pltpu.VMEM((SB_ROWS, 1), F32), pltpu.VMEM((SB_ROWS, W_GRP), F32)],
    )
    return pl.pallas_call(
        functools.partial(_sb_sample_kernel, layer=layer, n_pages=n_pages, page=page),
        grid_spec=grid_spec,
        out_shape=jax.ShapeDtypeStruct((db, SB_ROWS, HEAD_DIM), F32),
        compiler_params=_params(("arbitrary",)),
        name="sb_sample",
    )(page_table.reshape(-1), qbd, k_new, v_new, cache_k, cache_v)


def _row_fields(shape):
    r = _iota(shape, 0)
    qi = (r >> 1) & 3
    kvh = r & 1
    h = kvh * 4 + (r >> 3)
    slope = jnp.zeros(shape, F32)
    for hh in range(8):
        slope = jnp.where(h == hh, _head_slope(hh), slope)
    return qi, kvh, slope


def _pick_half(x, kvh64):
    return jnp.where(kvh64 == 0, x[:, :HEAD_DIM], x[:, HEAD_DIM:])


def _nsa_small_kernel(pt_ref, q_ref, kall_ref, vall_ref, kwc_ref, vwc_ref, kwn_ref, vwn_ref,
                      oc_ref, ow_ref, sel_ref, used_ref, kg_ref, vg_ref, *, n_pages, past):
    b = pl.program_id(0)

    def gather(p, carry):
        idx = pt_ref[b * n_pages + p]
        kg_ref[pl.ds(p, 1), :] = kall_ref[pl.ds(idx, 1), :]
        vg_ref[pl.ds(p, 1), :] = vall_ref[pl.ds(idx, 1), :]
        return carry

    lax.fori_loop(0, n_pages, gather, 0)

    q = q_ref[0].astype(BF16)
    rows = q_ref.shape[1]
    _, kvh64, _ = _row_fields((rows, HEAD_DIM))

    qi, _, slope = _row_fields((rows, n_pages))
    pidx = _iota((rows, n_pages), 1)
    qpos = past + qi
    per_page = kg_ref.shape[1] // LANES
    scores, oks = [], []
    for n4 in range(per_page):
        z = _dot_nt(q, kg_ref[:, n4 * LANES:(n4 + 1) * LANES].astype(BF16))
        dist = (qpos - ((pidx * per_page + n4) * L_CMP + (L_CMP - 1))).astype(F32)
        ok = dist >= 0
        oks.append(ok)
        scores.append(jnp.where(ok, z - slope * dist, NEG_BIG))
    m = functools.reduce(jnp.maximum, [jnp.max(s, axis=1, keepdims=True) for s in scores])
    es = [jnp.exp(s - m) for s in scores]
    tot = functools.reduce(jnp.add, [jnp.sum(e, axis=1, keepdims=True) for e in es])
    ps = [jnp.where(ok, e / tot, 0.0) for ok, e in zip(oks, es)]
    o_c = functools.reduce(jnp.add, [
        _dot(p.astype(BF16), vg_ref[:, n4 * LANES:(n4 + 1) * LANES].astype(BF16))
        for n4, p in enumerate(ps)])
    oc_ref[0] = _pick_half(o_c, kvh64)

    def group_sum(x):
        return x[0:8] + x[8:16] + x[16:24] + x[24:32]

    lane = _iota((8, n_pages), 1)
    s_even = group_sum(ps[0] + ps[1]) + jnp.where(lane == 0, FORCE_BONUS, 0.0)
    s_odd = group_sum(ps[2] + ps[3]) + jnp.where(lane == n_pages - 1, FORCE_BONUS, 0.0)
    score = jnp.concatenate([s_even, s_odd], axis=1)
    lane2 = _iota((8, 2 * n_pages), 1)
    blk = jnp.where(lane2 < n_pages, 2 * lane2, 2 * (lane2 - n_pages) + 1)
    sel = jnp.zeros((8, 2 * n_pages), F32)
    for _ in range(N_SEL - 1):
        best = jnp.max(score, axis=1, keepdims=True)
        first = jnp.min(jnp.where(score == best, blk, 2 ** 30), axis=1, keepdims=True)
        hit = (blk == first) & (best > -jnp.inf)
        sel = jnp.where(hit, 1.0, sel)
        score = jnp.where(blk == first, -jnp.inf, score)
    sel_ref[0] = sel
    any_row = jnp.max(sel, axis=0, keepdims=True)
    used = jnp.maximum(any_row[:, :n_pages], any_row[:, n_pages:])
    used_ref[0] = jnp.broadcast_to(used, (8, n_pages)).astype(jnp.int32)

    wbuf = kwc_ref.shape[4]
    n_new = kwn_ref.shape[1]
    padw = jnp.zeros((LANES - n_new, LANES), F32)
    kw_new = jnp.concatenate([kwn_ref[0], padw], axis=0).astype(BF16)
    vw_new = jnp.concatenate([vwn_ref[0], padw], axis=0).astype(BF16)
    kw_old = kwc_ref[0, 0].reshape(LANES, wbuf).astype(BF16)
    vw_old = vwc_ref[0, 0].reshape(LANES, wbuf).astype(BF16)

    def scores(z, dist):
        ok = (dist >= 0) & (dist < WINDOW)
        return jnp.where(ok, z - slope_of(dist.shape) * dist.astype(F32), NEG_BIG), ok

    def slope_of(shape):
        return _row_fields(shape)[2]

    qi_o, _, _ = _row_fields((rows, wbuf))
    qi_n, _, _ = _row_fields((rows, LANES))
    s_o, ok_o = scores(_dot(q, kw_old), wbuf + qi_o - _iota((rows, wbuf), 1))
    s_n, ok_n = scores(_dot_nt(q, kw_new), qi_n - _iota((rows, LANES), 1))
    m = jnp.maximum(jnp.max(s_o, axis=1, keepdims=True), jnp.max(s_n, axis=1, keepdims=True))
    e_o, e_n = jnp.exp(s_o - m), jnp.exp(s_n - m)
    tot = jnp.sum(e_o, axis=1, keepdims=True) + jnp.sum(e_n, axis=1, keepdims=True)
    p_o = jnp.where(ok_o, e_o / tot, 0.0).astype(BF16)
    p_n = jnp.where(ok_n, e_n / tot, 0.0).astype(BF16)
    ow_ref[0] = _pick_half(_dot_nt(p_o, vw_old) + _dot(p_n, vw_new), kvh64)


def _nsa_small(page_table, q_rows, kall, vall, win_k, win_v, kw_new, vw_new, layer, past):
    db, n_pages = page_table.shape
    rows = q_rows.shape[1]
    wbuf = win_k.shape[4]
    whole = lambda a: pl.BlockSpec(a.shape, lambda bi, pt: (0,) * a.ndim)
    per_b = lambda a: pl.BlockSpec((1,) + a.shape[1:], lambda bi, pt: (bi,) + (0,) * (a.ndim - 1))
    win = pl.BlockSpec((1, 1, 2, HEAD_DIM, wbuf), lambda bi, pt: (layer, bi, 0, 0, 0))
    grid_spec = pltpu.PrefetchScalarGridSpec(
        num_scalar_prefetch=1,
        grid=(db,),
        in_specs=[per_b(q_rows), whole(kall), whole(vall), win, win, per_b(kw_new), per_b(vw_new)],
        out_specs=[pl.BlockSpec((1, rows, HEAD_DIM), lambda bi, pt: (bi, 0, 0)),
                   pl.BlockSpec((1, rows, HEAD_DIM), lambda bi, pt: (bi, 0, 0)),
                   pl.BlockSpec((1, 8, 2 * n_pages), lambda bi, pt: (bi, 0, 0)),
                   pl.BlockSpec((1, 8, n_pages), lambda bi, pt: (bi, 0, 0))],
        scratch_shapes=[pltpu.VMEM((n_pages, kall.shape[1]), F32)] * 2,
    )
    return pl.pallas_call(
        functools.partial(_nsa_small_kernel, n_pages=n_pages, past=past),
        grid_spec=grid_spec,
        out_shape=[jax.ShapeDtypeStruct((db, rows, HEAD_DIM), F32),
                   jax.ShapeDtypeStruct((db, rows, HEAD_DIM), F32),
                   jax.ShapeDtypeStruct((db, 8, 2 * n_pages), F32),
                   jax.ShapeDtypeStruct((db, 8, n_pages), jnp.int32)],
        compiler_params=_params(("arbitrary",)),
        name="nsa_small_sample",
    )(page_table.reshape(-1), q_rows, kall, vall, win_k, win_v, kw_new, vw_new)


def _nsa_sel_kernel(pt_ref, used_ref, q_ref, sel_ref, kn_ref, vn_ref, oc_ref, ow_ref, br_ref, *rest,
                    npg, n_pages, page, past):
    del pt_ref
    krefs, vrefs = rest[:npg], rest[npg:2 * npg]
    o_ref, m_ref, l_ref, acc_ref = rest[2 * npg:]
    b = pl.program_id(0)
    step = pl.program_id(1)
    q = q_ref[0].astype(BF16)
    rows = q_ref.shape[1]
    qi, _, slope = _row_fields((rows, page))
    lane = _iota((rows, page), 1)
    qpos = past + qi

    def absorb(k, v, kpos0, mask, channel_major):
        kb, vb = k.astype(BF16), v.astype(BF16)
        dist = (qpos - (kpos0 + lane)).astype(F32)
        s = (_dot(q, kb) if channel_major else _dot_nt(q, kb)) - slope * dist
        if mask is None:
            mask = dist >= 0
        m_old = m_ref[...]
        m_new = jnp.maximum(m_old, jnp.max(jnp.where(mask, s, NEG_BIG), axis=1, keepdims=True))
        p = jnp.where(mask, jnp.exp(s - m_new), 0.0)
        alpha = jnp.exp(m_old - m_new)
        l_ref[...] = alpha * l_ref[...] + jnp.sum(p, axis=1, keepdims=True)
        pb = p.astype(BF16)
        acc_ref[...] = alpha * acc_ref[...] + (_dot_nt(pb, vb) if channel_major else _dot(pb, vb))
        m_ref[...] = m_new

    @pl.when(step == 0)
    def _():
        m_ref[...] = jnp.full_like(m_ref, NEG_BIG)
        l_ref[...] = jnp.zeros_like(l_ref)
        acc_ref[...] = jnp.zeros_like(acc_ref)
        pad = jnp.zeros((page - kn_ref.shape[1], LANES), F32)
        absorb(jnp.concatenate([kn_ref[0], pad], axis=0),
               jnp.concatenate([vn_ref[0], pad], axis=0), past, None, False)

    selm = jnp.concatenate([sel_ref[0]] * (rows // 8), axis=0)
    pl_lane = _iota((rows, n_pages), 1)
    per_blk = page // L_SEL
    for g in range(npg):
        pg = n_pages - 1 - (step * npg + g)

        @pl.when(used_ref[b * n_pages + pg] > 0)
        def _(g=g, pg=pg):
            mask = jnp.zeros((rows, page), jnp.bool_)
            for j in range(per_blk):
                cj = jnp.sum(jnp.where(pl_lane == pg, selm[:, j * n_pages:(j + 1) * n_pages], 0.0),
                             axis=1, keepdims=True)
                mask = mask | ((cj > 0.5) & (lane >= j * L_SEL) & (lane < (j + 1) * L_SEL))
            absorb(krefs[g][0, 0].reshape(LANES, page), vrefs[g][0, 0].reshape(LANES, page),
                   pg * page, mask, True)

    @pl.when(step == pl.num_programs(1) - 1)
    def _():
        _, kvh64, _ = _row_fields((rows, HEAD_DIM))
        o_s = _pick_half(acc_ref[...] / l_ref[...], kvh64)
        sig = _sigmoid(br_ref[0])
        o_ref[0] = sig[:, 0:1] * oc_ref[0] + sig[:, 1:2] * o_s + sig[:, 2:3] * ow_ref[0]


def _nsa_sel_sample(page_table, used, q_rows, sel, k_new, v_new, oc, ow, br_rows, cache_k, cache_v,
                    layer, past):
    db, n_pages = page_table.shape
    rows = q_rows.shape[1]
    page = cache_k.shape[4]
    npg = min(PAGES_PER_STEP, n_pages)
    steps = n_pages // npg

    def page_spec(g):
        def index(bi, s, pt, us):
            return (layer, pt[bi * n_pages + (n_pages - 1 - (s * npg + g))], 0, 0, 0)
        return pl.BlockSpec((1, 1, 2, HEAD_DIM, page), index)

    per_b = lambda a: pl.BlockSpec((1,) + a.shape[1:],
                                   lambda bi, s, pt, us: (bi,) + (0,) * (a.ndim - 1))
    grid_spec = pltpu.PrefetchScalarGridSpec(
        num_scalar_prefetch=2,
        grid=(db, steps),
        in_specs=[per_b(a) for a in (q_rows, sel, k_new, v_new, oc, ow, br_rows)]
        + [page_spec(g) for g in range(npg)] * 2,
        out_specs=pl.BlockSpec((1, rows, HEAD_DIM), lambda bi, s, pt, us: (bi, 0, 0)),
        scratch_shapes=[pltpu.VMEM((rows, 1), F32), pltpu.VMEM((rows, 1), F32),
                        pltpu.VMEM((rows, LANES), F32)],
    )
    return pl.pallas_call(
        functools.partial(_nsa_sel_kernel, npg=npg, n_pages=n_pages, page=page, past=past),
        grid_spec=grid_spec,
        out_shape=jax.ShapeDtypeStruct((db, rows, HEAD_DIM), F32),
        compiler_params=_params(("parallel", "arbitrary")),
        name="nsa_sel_sample",
    )(page_table.reshape(-1), used.reshape(-1), q_rows, sel, k_new, v_new, oc, ow, br_rows,
      *([cache_k] * npg), *([cache_v] * npg))


def _pack_w_in(w_in):
    depth, d, _ = w_in.shape
    wt = jnp.transpose(w_in, (0, 2, 1))
    pieces, src = [], 0
    placed = {}
    for name, width in _SRC:
        placed[name] = (src, width)
        src += width
    order = sorted(COL, key=COL.get)
    pos = 0
    for name in order:
        assert COL[name] == pos
        s, width = placed[name]
        pieces.append(wt[:, s:s + width, :])
        pos += width
    pieces.append(jnp.zeros((depth, N_PACK - pos, d), w_in.dtype))
    return jnp.concatenate(pieces, axis=1).astype(BF16)


def _expand_cmp_weight(w):
    depth, l, d, e = w.shape
    eye = jnp.eye(2, dtype=w.dtype)
    return jnp.einsum("zlde,hg->zlhdge", w, eye).reshape(depth, l * 2 * d, 2 * e).astype(BF16)


def _expand_cmp_weight_t(w, page):
    depth, l, d, e = w.shape
    n = page // l
    eye = jnp.eye(n, dtype=w.dtype)
    return jnp.einsum("zlde,nm->zdnlme", w, eye).reshape(depth, d, page, n * e).astype(BF16)


def _sample_rows(x, n_heads_last):
    db, tq = x.shape[:2]
    x = x.reshape(db, tq, 2, 4, n_heads_last)
    return x.transpose(0, 3, 1, 2, 4).reshape(db, 4 * tq * 2, n_heads_last)


def kernel(x_prompt, x_sample, cache_sb_k, cache_sb_v, cache_cmp_k, cache_cmp_v, cache_sel_k, cache_sel_v, cache_win_k, cache_win_v, state_pool, state_conv, page_table, norm_g, w_in, w_cmp_k, w_cmp_v, w_pool, pool_scale, conv_w, conv_b, ln_g, ln_b, w_pw, w_out, final_g):
    bp, seq, d_model = x_prompt.shape
    db, tq_s, _ = x_sample.shape
    depth, n_pool, page = cache_sb_k.shape[:3]
    assert tq_s == 4 and page == LANES and cache_win_k.shape[2] == WINDOW and seq % 256 == 0

    w = _prepare_weights(norm_g, w_in, w_cmp_k, w_cmp_v, w_pool, pool_scale, conv_w, conv_b,
                         ln_g, ln_b, w_pw, w_out, final_g)
    chan_major = lambda c: jnp.transpose(c, (0, 1, 3, 4, 2))
    sbk, sbv = chan_major(cache_sb_k), chan_major(cache_sb_v)
    selk, selv = chan_major(cache_sel_k), chan_major(cache_sel_v)
    cmpk = chan_major(cache_cmp_k).reshape(depth * n_pool * 2 * HEAD_DIM, page)
    cmpv = chan_major(cache_cmp_v).reshape(depth * n_pool * 2 * HEAD_DIM, page)

    hp = x_prompt.reshape(bp * seq, d_model)
    hs = x_sample.reshape(db * tq_s, d_model)
    p_states, s_states = [], []
    for l in range(depth):
        last = l == depth - 1
        hp, st = _prompt_layer(hp, bp, seq, w, l, last)
        p_states.append(st)
        hs, st = _sample_layer(hs, db, tq_s, w, l, last, page_table, sbk, sbv, cmpk, cmpv, selk, selv,
                               cache_win_k, cache_win_v, state_pool, state_conv)
        s_states.append(st)

    stacked = lambda states, j: jnp.stack([st[j] for st in states], axis=0)
    return (hp.reshape(bp, seq, d_model), hs.reshape(db, tq_s, d_model),
            *[stacked(p_states, j) for j in range(10)],
            *[stacked(s_states, j) for j in range(10)])


def _prepare_weights(norm_g, w_in, w_cmp_k, w_cmp_v, w_pool, pool_scale, conv_w, conv_b, ln_g, ln_b,
                     w_pw, w_out, final_g):
    depth = w_in.shape[0]
    vec3 = lambda a: a.reshape(depth, 1, -1)
    return dict(
        w_pack=_pack_w_in(w_in), w_out=w_out.astype(BF16),
        wk_exp=_expand_cmp_weight(w_cmp_k), wv_exp=_expand_cmp_weight(w_cmp_v),
        wk_t=_expand_cmp_weight_t(w_cmp_k, LANES), wv_t=_expand_cmp_weight_t(w_cmp_v, LANES),
        w_pool=w_pool.astype(BF16), w_pw=w_pw.astype(BF16),
        conv_w=jnp.pad(conv_w, ((0, 0), (0, CONV_HALO - CONV_W), (0, 0))),
        norm_g=vec3(norm_g), pool_scale=vec3(pool_scale), conv_b=vec3(conv_b),
        ln_g=vec3(ln_g), ln_b=vec3(ln_b), final_g=final_g.reshape(1, -1))


def _prompt_layer(hp, bp, seq, w, l, last):
    proj = _inproj(hp, w["norm_g"], w["w_pack"], l)
    proj3 = proj.reshape(bp, seq, N_PACK)
    col = lambda name, width: proj3[:, :, COL[name]:COL[name] + width]
    kc, vc = col("b_kc", LANES), col("b_vc", LANES)
    o_a = _sb_prompt(proj3)
    n_blocks = bp * seq // L_CMP
    kcmp, vcmp = _compress(kc.reshape(n_blocks, L_CMP * LANES), vc.reshape(n_blocks, L_CMP * LANES),
                           w["wk_exp"], w["wv_exp"], l, 0, n_blocks)
    n_cmp = seq // L_CMP
    o_cmp, sel = _nsa_cmp_prompt(proj3, kcmp.reshape(bp, n_cmp, LANES), vcmp.reshape(bp, n_cmp, LANES))
    o_b = _nsa_att_prompt(proj3, sel, o_cmp)
    zero_pool = jnp.zeros((bp, POOL_HALO, W_GRP), F32)
    zero_conv = jnp.zeros((bp, CONV_HALO, W_GRP), F32)
    o_c = _pool(proj3, COL["c_u"] // W_GRP, zero_pool, w["w_pool"], w["pool_scale"], l, 0)
    o_d, u_conv = _conv(proj3, COL["d_a"] // W_GRP, COL["d_gl"] // W_GRP, zero_conv,
                        w["conv_w"], w["conv_b"], w["ln_g"], w["ln_b"], w["w_pw"], l)
    flat = lambda a: a.reshape(bp * seq, W_GRP)
    hp_new = _outproj([flat(o_a), flat(o_b), flat(o_c), flat(o_d)], proj, hp, w["w_out"], l,
                      w["final_g"] if last else None)
    keep = min(WINDOW, seq)
    heads = lambda a, n: a.reshape(bp, -1, n, HEAD_DIM)
    states = (
        heads(col("a_k", W_GRP), 8), heads(col("a_v", W_GRP), 8),
        heads(kc, 2), heads(vc, 2),
        heads(col("b_ks", LANES), 2), heads(col("b_vs", LANES), 2),
        heads(col("b_kw", LANES)[:, seq - keep:], 2), heads(col("b_vw", LANES)[:, seq - keep:], 2),
        col("c_u", W_GRP)[:, seq - (POOL_HALO - 1):], u_conv[:, seq - (CONV_W - 1):])
    return hp_new, states


def _sample_layer(hs, db, tq_s, w, l, last, page_table, sbk, sbv, cmpk, cmpv, selk, selv,
                  cache_win_k, cache_win_v, state_pool, state_conv):
    depth, n_pool = sbk.shape[:2]
    page = sbk.shape[4]
    n_pages = page_table.shape[1]
    past = n_pages * page
    projs = _inproj(hs, w["norm_g"], w["w_pack"], l)
    projs3 = projs.reshape(db, tq_s, N_PACK)
    scol = lambda name, width: projs3[:, :, COL[name]:COL[name] + width]
    pad8 = lambda a: jnp.pad(a, ((0, 0), (0, 8 - tq_s), (0, 0)))
    head_of_col = jnp.arange(W_GRP) // HEAD_DIM
    sb_mask = (head_of_col[None, :] == jnp.arange(8)[:, None]).astype(F32)
    a_q = scol("a_q", W_GRP) * SCALE
    qbd = (a_q[:, :, None, :] * sb_mask[None, None]).reshape(db, tq_s * 8, W_GRP)
    o_a = _sb_sample(page_table, qbd, pad8(scol("a_k", W_GRP)), pad8(scol("a_v", W_GRP)), sbk, sbv, l)
    o_a = o_a.reshape(db * tq_s, W_GRP)
    kall = _compress_cache(cmpk, w["wk_t"], l, n_pool * 2)
    vall = _compress_cache(cmpv, w["wv_t"], l, n_pool * 2)
    per_page = page // L_CMP
    relane = lambda a: a.reshape(n_pool, 2, per_page, HEAD_DIM).transpose(0, 2, 1, 3).reshape(
        n_pool, per_page * LANES)
    kall, vall = relane(kall), relane(vall)
    b_q = _sample_rows(scol("b_q", W_GRP) * SCALE, HEAD_DIM)
    kvh_of_row = jnp.arange(b_q.shape[1]) & 1
    q_rows = (b_q[:, :, None, :] * jnp.eye(2, dtype=F32)[kvh_of_row][None, :, :, None]).reshape(db, -1, LANES)
    br_rows = jnp.pad(_sample_rows(scol("b_br", 24), 3), ((0, 0), (0, 0), (0, LANES - 3)))
    kw_new, vw_new = scol("b_kw", LANES), scol("b_vw", LANES)
    wink = jnp.transpose(cache_win_k, (0, 1, 3, 4, 2))
    winv = jnp.transpose(cache_win_v, (0, 1, 3, 4, 2))
    o_cmp, o_win, sel, used = _nsa_small(page_table, q_rows, kall, vall, wink, winv,
                                         pad8(kw_new), pad8(vw_new), l, past)
    o_b_rows = _nsa_sel_sample(page_table, used[:, 0], q_rows, sel, pad8(scol("b_ks", LANES)),
                               pad8(scol("b_vs", LANES)), o_cmp, o_win, br_rows, selk, selv, l, past)
    o_b = o_b_rows.reshape(db, 4, tq_s, 2, HEAD_DIM).transpose(0, 2, 3, 1, 4).reshape(db * tq_s, W_GRP)
    pool_pre = jnp.pad(state_pool[l], ((0, 0), (POOL_HALO - state_pool.shape[2], 0), (0, 0)))
    o_c = _pool(pad8(scol("c_u", W_GRP)), 0, pool_pre, w["w_pool"], w["pool_scale"], l, past)
    o_c = o_c[:, :tq_s].reshape(db * tq_s, W_GRP)
    conv_pre = jnp.pad(state_conv[l], ((0, 0), (CONV_HALO - state_conv.shape[2], 0), (0, 0)))
    o_d, u_conv = _conv(pad8(scol("d_a", 2 * W_GRP)), 0, 1, conv_pre,
                        w["conv_w"], w["conv_b"], w["ln_g"], w["ln_b"], w["w_pw"], l)
    o_d = o_d[:, :tq_s].reshape(db * tq_s, W_GRP)
    hs_new = _outproj([o_a, o_b, o_c, o_d], projs, hs, w["w_out"], l, w["final_g"] if last else None)
    sheads = lambda a, n: a.reshape(db, -1, n, HEAD_DIM)
    states = (
        sheads(scol("a_k", W_GRP), 8), sheads(scol("a_v", W_GRP), 8),
        sheads(scol("b_kc", LANES), 2), sheads(scol("b_vc", LANES), 2),
        sheads(scol("b_ks", LANES), 2), sheads(scol("b_vs", LANES), 2),
        jnp.concatenate([cache_win_k[l][:, tq_s:], sheads(kw_new, 2)], axis=1),
        jnp.concatenate([cache_win_v[l][:, tq_s:], sheads(vw_new, 2)], axis=1),
        jnp.concatenate([state_pool[l][:, tq_s:], scol("c_u", W_GRP)], axis=1),
        jnp.concatenate([state_conv[l][:, tq_s:], u_conv[:, :tq_s]], axis=1))
    return hs_new, states
```
